```python
import jax, jax.numpy as jnp
from jax import lax
import numpy as np

D_MODEL = 4096
BATCH = 4
SEQ = 2048
DEPTH = 2
DEC_BATCH = 32
DEC_SEQ = 64
PAST_LEN = 4096

CHUNK = 64
N_EVEN = (DEPTH + 1) // 2
N_ODD = DEPTH // 2
CONV_DIM = D_MODEL // 2
CONV_W = 3
N_HEADS = 32
N_KV = 8
HEAD_DIM = 64
Q_DIM = N_HEADS * HEAD_DIM
KV_DIM = N_KV * HEAD_DIM
WINDOW = 128
WIN_CHUNKS = WINDOW // CHUNK
IN_AB_DIM = 3 * CONV_DIM + Q_DIM + 2 * KV_DIM
MIX_AB_DIM = CONV_DIM + Q_DIM
SPLIT_AB = (CONV_DIM, 2 * CONV_DIM, 3 * CONV_DIM, 3 * CONV_DIM + Q_DIM, 3 * CONV_DIM + Q_DIM + KV_DIM)
GMLP_CHUNK = 128
GMLP_DIM = D_MODEL
GMLP_GROUPS = 16
GMLP_GDIM = GMLP_DIM // GMLP_GROUPS
D_FF = 11008
FFN_CONV_W = 3
RMS_EPS = 1e-6
LN_EPS = 1e-5
NEG_INF = -1e30

kernel_name = 'chunk_causal_hybrid_conv_swa_gmlp_step'


def alibi_slopes():
    exps = np.arange(1, N_HEADS + 1, dtype=np.float32) * (8.0 / N_HEADS)
    return jnp.asarray(np.power(2.0, -exps).astype(np.float32))


def rmsnorm(x, g):
    xf = x.astype(jnp.float32)
    y = xf * lax.rsqrt(jnp.mean(xf * xf, axis=-1, keepdims=True) + RMS_EPS)
    return (y * g.astype(jnp.float32)).astype(x.dtype)


def layernorm(x, g, b):
    xf = x.astype(jnp.float32)
    mu = jnp.mean(xf, axis=-1, keepdims=True)
    var = jnp.mean(jnp.square(xf - mu), axis=-1, keepdims=True)
    y = (xf - mu) * lax.rsqrt(var + LN_EPS) * g.astype(jnp.float32) + b.astype(jnp.float32)
    return y.astype(x.dtype)


def causal_dwconv(x, prev, w):
    t = x.shape[1]
    xp = jnp.concatenate([prev.astype(x.dtype), x], axis=1)
    out = xp[:, 0:t] * w[0]
    for j in range(1, w.shape[0]):
        out = out + xp[:, j:j + t] * w[j]
    return out, xp[:, t:]


def attention_core(q, k, v, q_pos, k_pos, sinks):
    bsz, n, t = q.shape[0], q.shape[1], q.shape[2]
    g = N_HEADS // N_KV
    qg = q.reshape(bsz, n, t, N_KV, g, HEAD_DIM)
    scores = jnp.einsum('bntkgd,bnskd->bnkgts', qg, k,
                        preferred_element_type=jnp.float32) * (HEAD_DIM ** -0.5)
    slopes = alibi_slopes().reshape(N_KV, g)
    dist = jnp.abs(q_pos[:, :, None] - k_pos[:, None, :]).astype(jnp.float32)
    bias = -slopes[None, :, :, None, None] * dist[:, None, None]
    valid = (k_pos >= 0)[:, None, None, None, :]
    scores = jnp.where(valid[None], scores + bias[None], NEG_INF)
    sink = sinks.astype(jnp.float32).reshape(1, 1, N_KV, g, 1, 1)
    m = jnp.maximum(jnp.max(scores, axis=-1, keepdims=True), sink)
    p = jnp.exp(scores - m)
    denom = jnp.sum(p, axis=-1, keepdims=True) + jnp.exp(sink - m)
    out = jnp.einsum('bnkgts,bnskd->bntkgd', p / denom, v.astype(jnp.float32))
    return out.reshape(bsz, n * t, N_HEADS * HEAD_DIM).astype(q.dtype)


def swa_prompt(q, k, v, sinks):
    bsz, s = q.shape[0], q.shape[1]
    nc = s // CHUNK
    qc = q.reshape(bsz, nc, CHUNK, N_HEADS, HEAD_DIM)
    kr = k.reshape(bsz, s, N_KV, HEAD_DIM)
    vr = v.reshape(bsz, s, N_KV, HEAD_DIM)
    pad = jnp.zeros((bsz, WIN_CHUNKS, CHUNK, N_KV, HEAD_DIM), k.dtype)
    kp = jnp.concatenate([pad, kr.reshape(bsz, nc, CHUNK, N_KV, HEAD_DIM)], axis=1)
    vp = jnp.concatenate([pad, vr.reshape(bsz, nc, CHUNK, N_KV, HEAD_DIM)], axis=1)
    kb = jnp.concatenate([kp[:, j:j + nc] for j in range(WIN_CHUNKS + 1)], axis=2)
    vb = jnp.concatenate([vp[:, j:j + nc] for j in range(WIN_CHUNKS + 1)], axis=2)
    c = jnp.arange(nc, dtype=jnp.int32)[:, None]
    q_pos = c * CHUNK + jnp.arange(CHUNK, dtype=jnp.int32)[None]
    k_pos = (c - WIN_CHUNKS) * CHUNK + jnp.arange((WIN_CHUNKS + 1) * CHUNK, dtype=jnp.int32)[None]
    out = attention_core(qc, kb, vb, q_pos, k_pos, sinks)
    return out, kr[:, s - WINDOW:], vr[:, s - WINDOW:]


def swa_sample(q, k, v, k_cache, v_cache, sinks):
    bsz, t = q.shape[0], q.shape[1]
    qc = q.reshape(bsz, 1, t, N_HEADS, HEAD_DIM)
    k_all = jnp.concatenate([k_cache.astype(k.dtype), k.reshape(bsz, t, N_KV, HEAD_DIM)], axis=1)
    v_all = jnp.concatenate([v_cache.astype(v.dtype), v.reshape(bsz, t, N_KV, HEAD_DIM)], axis=1)
    q_pos = (PAST_LEN + jnp.arange(t, dtype=jnp.int32))[None]
    k_pos = (PAST_LEN - WINDOW + jnp.arange(WINDOW + t, dtype=jnp.int32))[None]
    out = attention_core(qc, k_all[:, None], v_all[:, None], q_pos, k_pos, sinks)
    return out, k_all[:, t:], v_all[:, t:]


def mixer_ab(h, conv_prev, k_cache, v_cache, w_in, conv_w, sinks, w_out):
    proj = h @ w_in
    gate_b, gate_c, hv, q, k, v = jnp.split(proj, SPLIT_AB, axis=-1)
    a, conv_state = causal_dwconv(gate_c * hv, conv_prev, conv_w)
    a = gate_b * a
    if k_cache is None:
        attn, k_state, v_state = swa_prompt(q, k, v, sinks)
    else:
        attn, k_state, v_state = swa_sample(q, k, v, k_cache, v_cache, sinks)
    y = jnp.concatenate([a, attn], axis=-1) @ w_out
    return y, conv_state, k_state, v_state


def gmlp_mixer(h, w_in, ln_g, ln_b, w_s, b_s, w_out):
    bsz, t = h.shape[0], h.shape[1]
    z = jax.nn.gelu(h @ w_in, approximate=False)
    u, v = jnp.split(z, 2, axis=-1)
    v = layernorm(v, ln_g, ln_b)
    cl = min(t, GMLP_CHUNK)
    n = t // cl
    vc = v.reshape(bsz, n, cl, GMLP_GROUPS, GMLP_GDIM)
    ws = jnp.tril(w_s[:, :cl, :cl])
    s = jnp.einsum('gts,bnsgc->bntgc', ws, vc) + b_s[:, :cl].T[None, None, :, :, None]
    y = (u * s.reshape(bsz, t, GMLP_DIM)) @ w_out
    return y, v


def conv_ffn(h, prev, w_up, conv_w, conv_b, w_down):
    up = h @ w_up
    c, state = causal_dwconv(up, prev, conv_w)
    g, u = jnp.split(c + conv_b, 2, axis=-1)
    return (jax.nn.silu(g) * u) @ w_down, state


def trunk(x, conv_a_cache, swa_k_cache, swa_v_cache, ffn_cache, norm_mix, norm_ffn, norm_final,
          w_in_ab, conv_w_a, sinks, w_out_ab, w_in_c, ln_g_c, ln_b_c, w_s_c, b_s_c, w_out_c,
          w_up_ffn, conv_w_ffn, conv_b_ffn, w_down_ffn):
    sample = ffn_cache is not None
    bsz = x.shape[0]
    conv_states, k_states, v_states, ffn_states, gv_states = [], [], [], [], []
    for layer in range(DEPTH):
        i = layer // 2
        h = rmsnorm(x, norm_mix[layer])
        if layer % 2 == 0:
            if sample:
                prev, kc, vc = conv_a_cache[i], swa_k_cache[i], swa_v_cache[i]
            else:
                prev, kc, vc = jnp.zeros((bsz, CONV_W - 1, CONV_DIM), x.dtype), None, None
            y, cs, ks, vs = mixer_ab(h, prev, kc, vc, w_in_ab[i], conv_w_a[i], sinks[i], w_out_ab[i])
            conv_states.append(cs)
            k_states.append(ks)
            v_states.append(vs)
        else:
            y, gv = gmlp_mixer(h, w_in_c[i], ln_g_c[i], ln_b_c[i], w_s_c[i], b_s_c[i], w_out_c[i])
            gv_states.append(gv)
        x = x + y
        h = rmsnorm(x, norm_ffn[layer])
        fprev = ffn_cache[layer] if sample else jnp.zeros((bsz, FFN_CONV_W - 1, 2 * D_FF), x.dtype)
        y, fs = conv_ffn(h, fprev, w_up_ffn[layer], conv_w_ffn[layer], conv_b_ffn[layer], w_down_ffn[layer])
        ffn_states.append(fs)
        x = x + y
    x = rmsnorm(x, norm_final)
    return (x, jnp.stack(conv_states), jnp.stack(k_states), jnp.stack(v_states),
            jnp.stack(ffn_states), jnp.stack(gv_states))


def setup_inputs(seed: int = 0) -> dict:
    key = jax.random.key(seed)
    ks = jax.random.split(key, 24)

    def nrm(k, shape, scale):
        return jax.random.normal(k, shape, jnp.float32) * scale

    return {
        'x_prompt': nrm(ks[0], (BATCH, SEQ, D_MODEL), 1.0),
        'x_sample': nrm(ks[1], (DEC_BATCH, DEC_SEQ, D_MODEL), 1.0),
        'state_conv_a': nrm(ks[2], (N_EVEN, DEC_BATCH, CONV_W - 1, CONV_DIM), 1.0),
        'cache_swa_k': nrm(ks[3], (N_EVEN, DEC_BATCH, WINDOW, N_KV, HEAD_DIM), 1.0),
        'cache_swa_v': nrm(ks[4], (N_EVEN, DEC_BATCH, WINDOW, N_KV, HEAD_DIM), 1.0),
        'state_ffn_conv': nrm(ks[5], (DEPTH, DEC_BATCH, FFN_CONV_W - 1, 2 * D_FF), 1.0),
        'norm_mix': 1.0 + nrm(ks[6], (DEPTH, D_MODEL), 0.05),
        'norm_ffn': 1.0 + nrm(ks[7], (DEPTH, D_MODEL), 0.05),
        'norm_final': 1.0 + nrm(ks[8], (D_MODEL,), 0.05),
        'w_in_ab': nrm(ks[9], (N_EVEN, D_MODEL, IN_AB_DIM), D_MODEL ** -0.5),
        'conv_w_a': nrm(ks[10], (N_EVEN, CONV_W, CONV_DIM), CONV_W ** -0.5),
        'sinks': nrm(ks[11], (N_EVEN, N_HEADS), 0.5),
        'w_out_ab': nrm(ks[12], (N_EVEN, MIX_AB_DIM, D_MODEL), MIX_AB_DIM ** -0.5),
        'w_in_c': nrm(ks[13], (N_ODD, D_MODEL, 2 * GMLP_DIM), D_MODEL ** -0.5),
        'ln_g_c': 1.0 + nrm(ks[14], (N_ODD, GMLP_DIM), 0.05),
        'ln_b_c': nrm(ks[15], (N_ODD, GMLP_DIM), 0.02),
        'w_s_c': nrm(ks[16], (N_ODD, GMLP_GROUPS, GMLP_CHUNK, GMLP_CHUNK), GMLP_CHUNK ** -0.5),
        'b_s_c': 1.0 + nrm(ks[17], (N_ODD, GMLP_GROUPS, GMLP_CHUNK), 0.1),
        'w_out_c': nrm(ks[18], (N_ODD, GMLP_DIM, D_MODEL), GMLP_DIM ** -0.5),
        'w_up_ffn': nrm(ks[19], (DEPTH, D_MODEL, 2 * D_FF), D_MODEL ** -0.5),
        'conv_w_ffn': nrm(ks[20], (DEPTH, FFN_CONV_W, 2 * D_FF), FFN_CONV_W ** -0.5),
        'conv_b_ffn': nrm(ks[21], (DEPTH, 2 * D_FF), 0.02),
        'w_down_ffn': nrm(ks[22], (DEPTH, D_FF, D_MODEL), D_FF ** -0.5),
    }


def reference(x_prompt, x_sample, state_conv_a, cache_swa_k, cache_swa_v, state_ffn_conv,
              norm_mix, norm_ffn, norm_final, w_in_ab, conv_w_a, sinks, w_out_ab,
              w_in_c, ln_g_c, ln_b_c, w_s_c, b_s_c, w_out_c,
              w_up_ffn, conv_w_ffn, conv_b_ffn, w_down_ffn):
    y_p, conv_p, k_p, v_p, ffn_p, _ = trunk(
        x_prompt, None, None, None, None, norm_mix, norm_ffn, norm_final,
        w_in_ab, conv_w_a, sinks, w_out_ab, w_in_c, ln_g_c, ln_b_c, w_s_c, b_s_c, w_out_c,
        w_up_ffn, conv_w_ffn, conv_b_ffn, w_down_ffn)
    y_s, conv_s, k_s, v_s, ffn_s, gv_s = trunk(
        x_sample, state_conv_a, cache_swa_k, cache_swa_v, state_ffn_conv, norm_mix, norm_ffn, norm_final,
        w_in_ab, conv_w_a, sinks, w_out_ab, w_in_c, ln_g_c, ln_b_c, w_s_c, b_s_c, w_out_c,
        w_up_ffn, conv_w_ffn, conv_b_ffn, w_down_ffn)
    return (y_p, y_s, conv_p, conv_s, k_p, k_s, v_p, v_s, ffn_p, ffn_s, gv_s)
```

```python
import functools

import numpy as np
import jax
import jax.numpy as jnp
from jax import lax
from jax.experimental import pallas as pl
from jax.experimental.pallas import tpu as pltpu

F32 = jnp.float32
BF16 = jnp.bfloat16

D_MODEL = 4096
CHUNK = 64
CONV_DIM = 2048
N_HEADS = 32
N_KV = 8
HEAD_DIM = 64
Q_DIM = N_HEADS * HEAD_DIM
KV_DIM = N_KV * HEAD_DIM
WINDOW = 128
IN_AB_DIM = 3 * CONV_DIM + Q_DIM + 2 * KV_DIM
GMLP_CHUNK = 128
GMLP_GROUPS = 16
GMLP_GDIM = D_MODEL // GMLP_GROUPS
D_FF = 11008
RMS_EPS = 1e-6
LN_EPS = 1e-5
NEG_INF = -1e30

V7X_VMEM_BYTES = 64 * 1024 * 1024
VMEM_LIMIT = V7X_VMEM_BYTES - 8 * 1024 * 1024

ROWS = 2048
SUB_ROWS = 512
FF_COLS = 256
N_FF_TILES = D_FF // FF_COLS


def _params(*sem):
    return pltpu.CompilerParams(dimension_semantics=sem, vmem_limit_bytes=VMEM_LIMIT)


def _dot(a, b):
    return jnp.dot(a, b, preferred_element_type=F32)


def _rmsnorm_kernel(x_ref, g_ref, o_ref):
    x = x_ref[...]
    r = lax.rsqrt(jnp.mean(x * x, axis=-1, keepdims=True) + RMS_EPS)
    o_ref[...] = (x * r * g_ref[...]).astype(o_ref.dtype)


def _rmsnorm(x, g, out_dtype, tm=512):
    m, d = x.shape
    return pl.pallas_call(
        _rmsnorm_kernel,
        grid=(m // tm,),
        in_specs=[pl.BlockSpec((tm, d), lambda i: (i, 0)),
                  pl.BlockSpec((1, d), lambda i: (0, 0))],
        out_specs=pl.BlockSpec((tm, d), lambda i: (i, 0)),
        out_shape=jax.ShapeDtypeStruct((m, d), out_dtype),
        compiler_params=_params("parallel"),
        name="rmsnorm",
    )(x, g.reshape(1, d))


def _gelu(x):
    return 0.5 * x * (1.0 + lax.erf(x * np.float32(np.sqrt(0.5))))


def _mm_kernel(*refs, n_lhs, epilogue):
    acc = _dot(refs[0][...], refs[n_lhs][...])
    for l in range(1, n_lhs):
        acc = acc + _dot(refs[l][...], refs[n_lhs + l][...])
    rest = refs[2 * n_lhs:]
    if epilogue == "residual":
        res_ref, o_ref = rest
        o_ref[...] = res_ref[...] + acc
    elif epilogue == "gelu":
        (o_ref,) = rest
        o_ref[...] = _gelu(acc)
    else:
        (o_ref,) = rest
        o_ref[...] = acc


def _matmul(lhs_list, w, *, tm, tn, epilogue="none", residual=None, name):
    m = lhs_list[0].shape[0]
    n = w.shape[1]
    in_specs, args = [], []
    for l in lhs_list:
        in_specs.append(pl.BlockSpec((tm, l.shape[1]), lambda i, j: (i, 0),
                                     pipeline_mode=pl.Buffered(1)))
        args.append(l)
    row_blk = 0
    for l in lhs_list:
        k = l.shape[1]
        in_specs.append(pl.BlockSpec((k, tn), functools.partial(lambda i, j, rb: (rb, j), rb=row_blk // k)))
        args.append(w)
        row_blk += k
    if residual is not None:
        in_specs.append(pl.BlockSpec((tm, tn), lambda i, j: (i, j)))
        args.append(residual)
    return pl.pallas_call(
        functools.partial(_mm_kernel, n_lhs=len(lhs_list), epilogue=epilogue),
        grid=(m // tm, n // tn),
        in_specs=in_specs,
        out_specs=pl.BlockSpec((tm, tn), lambda i, j: (i, j)),
        out_shape=jax.ShapeDtypeStruct((m, n), F32),
        compiler_params=_params("parallel", "arbitrary"),
        name=name,
    )(*args)


def _causal_conv3(x, prev, w, seq_len):
    rows, c = x.shape
    nseq = rows // seq_len
    t = lax.broadcasted_iota(jnp.int32, (rows, 1), 0) & (seq_len - 1)
    p0 = jnp.broadcast_to(prev[:, 0:1, :], (nseq, seq_len, c)).reshape(rows, c)
    p1 = jnp.broadcast_to(prev[:, 1:2, :], (nseq, seq_len, c)).reshape(rows, c)
    x1 = jnp.where(t == 0, p1, pltpu.roll(x, 1, 0))
    x2 = jnp.where(t == 0, p0, jnp.where(t == 1, p1, pltpu.roll(x, 2, 0)))
    return x2 * w[0:1, :] + x1 * w[1:2, :] + x * w[2:3, :]


def _last_two_rows(x, seq_len):
    rows, c = x.shape
    return x.reshape(rows // seq_len, seq_len, c)[:, seq_len - 2:, :]


def _conv_a_kernel(gb_ref, gc_ref, hv_ref, w_ref, prev_ref, a_ref, st_ref, *, seq_len):
    xin = gc_ref[...] * hv_ref[...]
    conv = _causal_conv3(xin, prev_ref[...], w_ref[...], seq_len)
    a_ref[...] = (gb_ref[...] * conv).astype(a_ref.dtype)
    st_ref[...] = _last_two_rows(xin, seq_len)


def _conv_a(proj, conv_w, prev, seq_len, tc=256):
    m = proj.shape[0]
    nseq_tile = ROWS // seq_len
    nj = CONV_DIM // tc
    return pl.pallas_call(
        functools.partial(_conv_a_kernel, seq_len=seq_len),
        grid=(m // ROWS, nj),
        in_specs=[pl.BlockSpec((ROWS, tc), lambda i, j: (i, j)),
                  pl.BlockSpec((ROWS, tc), lambda i, j: (i, j + nj)),
                  pl.BlockSpec((ROWS, tc), lambda i, j: (i, j + 2 * nj)),
                  pl.BlockSpec((3, tc), lambda i, j: (0, j)),
                  pl.BlockSpec((nseq_tile, 2, tc), lambda i, j: (i, 0, j))],
        out_specs=[pl.BlockSpec((ROWS, tc), lambda i, j: (i, j)),
                   pl.BlockSpec((nseq_tile, 2, tc), lambda i, j: (i, 0, j))],
        out_shape=[jax.ShapeDtypeStruct((m, CONV_DIM), BF16),
                   jax.ShapeDtypeStruct((m // seq_len, 2, CONV_DIM), F32)],
        compiler_params=_params("parallel", "parallel"),
        name="conv_a",
    )(proj, proj, proj, conv_w, prev)


def _alibi_slopes():
    exps = np.arange(1, N_HEADS + 1, dtype=np.float32) * (8.0 / N_HEADS)
    return np.power(2.0, -exps).astype(np.float32)


def _attn_kernel(sink_ref, q_ref, ka_ref, kb_ref, kc_ref, va_ref, vb_ref, vc_ref, o_ref, o_acc,
                 *, mask_history):
    nk = 3 * CHUNK
    t = lax.broadcasted_iota(jnp.int32, (CHUNK, nk), 0)
    s = lax.broadcasted_iota(jnp.int32, (CHUNK, nk), 1)
    dist = jnp.abs(t + (WINDOW - s)).astype(F32)
    if mask_history:
        valid = s >= (2 - pl.program_id(1)) * CHUNK
    slopes = _alibi_slopes()
    group = N_HEADS // N_KV
    for kh in range(N_KV):
        cols = slice(kh * HEAD_DIM, (kh + 1) * HEAD_DIM)
        k_all = jnp.concatenate([ka_ref[:, cols], kb_ref[:, cols], kc_ref[:, cols]], axis=0).astype(BF16)
        v_all = jnp.concatenate([va_ref[:, cols], vb_ref[:, cols], vc_ref[:, cols]], axis=0).astype(BF16)
        for g in range(group):
            h = kh * group + g
            hcols = slice(h * HEAD_DIM, (h + 1) * HEAD_DIM)
            q = q_ref[:, hcols].astype(BF16)
            sc = lax.dot_general(q, k_all, (((1,), (1,)), ((), ())), preferred_element_type=F32)
            sc = sc * (HEAD_DIM ** -0.5) + (-float(slopes[h])) * dist
            if mask_history:
                sc = jnp.where(valid, sc, NEG_INF)
            sink = sink_ref[h]
            m = jnp.maximum(jnp.max(sc, axis=-1, keepdims=True), sink)
            p = jnp.exp(sc - m)
            denom = jnp.sum(p, axis=-1, keepdims=True) + jnp.exp(sink - m)
            wts = (p * (1.0 / denom)).astype(BF16)
            o_acc[:, hcols] = _dot(wts, v_all)
    o_ref[...] = o_acc[...].astype(o_ref.dtype)


def _attention(q_src, q_blk, k_hist, v_hist, hist_maps, k_new, v_new, new_maps, sinks, *, mask_history, grid):
    m = q_src.shape[0]
    kv = (CHUNK, KV_DIM)
    in_specs = [pl.BlockSpec(memory_space=pltpu.SMEM),
                pl.BlockSpec((CHUNK, Q_DIM), q_blk),
                pl.BlockSpec(kv, hist_maps[0]), pl.BlockSpec(kv, hist_maps[1]), pl.BlockSpec(kv, new_maps[0]),
                pl.BlockSpec(kv, hist_maps[2]), pl.BlockSpec(kv, hist_maps[3]), pl.BlockSpec(kv, new_maps[1])]
    nchunk = grid[1]
    return pl.pallas_call(
        functools.partial(_attn_kernel, mask_history=mask_history),
        grid=grid,
        in_specs=in_specs,
        out_specs=pl.BlockSpec((CHUNK, Q_DIM), lambda b, c: (b * nchunk + c, 0)),
        out_shape=jax.ShapeDtypeStruct((m, Q_DIM), BF16),
        scratch_shapes=[pltpu.VMEM((CHUNK, Q_DIM), F32)],
        compiler_params=_params("parallel", "parallel"),
        name="swa",
    )(sinks, q_src, k_hist, k_hist, k_new, v_hist, v_hist, v_new)


def _gmlp_kernel(u_ref, v_ref, lng_ref, lnb_ref, ws_ref, bs_ref, o_ref, vn_ref, *, cl):
    v = v_ref[...]
    mu = jnp.mean(v, axis=-1, keepdims=True)
    var = jnp.mean(jnp.square(v - mu), axis=-1, keepdims=True)
    vn = (v - mu) * lax.rsqrt(var + LN_EPS) * lng_ref[...] + lnb_ref[...]
    vn_ref[...] = vn
    row = lax.broadcasted_iota(jnp.int32, (cl, cl), 0)
    col = lax.broadcasted_iota(jnp.int32, (cl, cl), 1)
    lower = row >= col
    for g in range(GMLP_GROUPS):
        cols = slice(g * GMLP_GDIM, (g + 1) * GMLP_GDIM)
        ws = jnp.where(lower, ws_ref[g, 0:cl, 0:cl], 0.0).astype(BF16)
        s = _dot(ws, vn[:, cols].astype(BF16)) + bs_ref[0:cl, g:g + 1]
        o_ref[:, cols] = (u_ref[:, cols] * s).astype(o_ref.dtype)


def _gmlp_gate(z, ln_g, ln_b, w_s, b_s_t, cl):
    m = z.shape[0]
    return pl.pallas_call(
        functools.partial(_gmlp_kernel, cl=cl),
        grid=(m // cl,),
        in_specs=[pl.BlockSpec((cl, D_MODEL), lambda i: (i, 0)),
                  pl.BlockSpec((cl, D_MODEL), lambda i: (i, 1)),
                  pl.BlockSpec((1, D_MODEL), lambda i: (0, 0)),
                  pl.BlockSpec((1, D_MODEL), lambda i: (0, 0)),
                  pl.BlockSpec((GMLP_GROUPS, GMLP_CHUNK, GMLP_CHUNK), lambda i: (0, 0, 0)),
                  pl.BlockSpec((GMLP_CHUNK, GMLP_GROUPS), lambda i: (0, 0))],
        out_specs=[pl.BlockSpec((cl, D_MODEL), lambda i: (i, 0)),
                   pl.BlockSpec((cl, D_MODEL), lambda i: (i, 0))],
        out_shape=[jax.ShapeDtypeStruct((m, D_MODEL), BF16),
                   jax.ShapeDtypeStruct((m, D_MODEL), F32)],
        compiler_params=_params("parallel"),
        name="gmlp_gate",
    )(z, z, ln_g.reshape(1, -1), ln_b.reshape(1, -1), w_s, b_s_t)


def _ffn_up_kernel(h_ref, wg_ref, wu_ref, cwg_ref, cwu_ref, bg_ref, bu_ref, pg_ref, pu_ref,
                   act_ref, sg_ref, su_ref, *, seq_len):
    sub_len = min(seq_len, SUB_ROWS)
    nseq_sub = SUB_ROWS // sub_len
    prev_g = prev_u = None
    for r in range(ROWS // SUB_ROWS):
        rows = slice(r * SUB_ROWS, (r + 1) * SUB_ROWS)
        h = h_ref[rows, :]
        up_g = _dot(h, wg_ref[...])
        up_u = _dot(h, wu_ref[...])
        if seq_len >= ROWS and r > 0:
            pass
        else:
            seqs = slice(r * nseq_sub, (r + 1) * nseq_sub)
            prev_g, prev_u = pg_ref[seqs], pu_ref[seqs]
        cg = _causal_conv3(up_g, prev_g, cwg_ref[...], sub_len) + bg_ref[...]
        cu = _causal_conv3(up_u, prev_u, cwu_ref[...], sub_len) + bu_ref[...]
        act_ref[rows, :] = (cg * jax.nn.sigmoid(cg) * cu).astype(act_ref.dtype)
        prev_g, prev_u = _last_two_rows(up_g, sub_len), _last_two_rows(up_u, sub_len)
        if seq_len >= ROWS:
            if r == ROWS // SUB_ROWS - 1:
                sg_ref[...] = prev_g
                su_ref[...] = prev_u
        else:
            sg_ref[seqs] = prev_g
            su_ref[seqs] = prev_u


def _ffn_up(h, w_up, conv_w, conv_b, prev, seq_len):
    m = h.shape[0]
    nseq_tile = ROWS // seq_len
    nj = N_FF_TILES
    st_spec_g = pl.BlockSpec((nseq_tile, 2, FF_COLS), lambda i, j: (i, 0, j))
    st_spec_u = pl.BlockSpec((nseq_tile, 2, FF_COLS), lambda i, j: (i, 0, j + nj))
    conv_b = conv_b.reshape(1, -1)
    return pl.pallas_call(
        functools.partial(_ffn_up_kernel, seq_len=seq_len),
        grid=(m // ROWS, nj),
        in_specs=[pl.BlockSpec((ROWS, D_MODEL), lambda i, j: (i, 0), pipeline_mode=pl.Buffered(1)),
                  pl.BlockSpec((D_MODEL, FF_COLS), lambda i, j: (0, j)),
                  pl.BlockSpec((D_MODEL, FF_COLS), lambda i, j: (0, j + nj)),
                  pl.BlockSpec((3, FF_COLS), lambda i, j: (0, j)),
                  pl.BlockSpec((3, FF_COLS), lambda i, j: (0, j + nj)),
                  pl.BlockSpec((1, FF_COLS), lambda i, j: (0, j)),
                  pl.BlockSpec((1, FF_COLS), lambda i, j: (0, j + nj)),
                  st_spec_g, st_spec_u],
        out_specs=[pl.BlockSpec((ROWS, FF_COLS), lambda i, j: (i, j)),
                   pl.BlockSpec((nseq_tile, 2, FF_COLS), lambda i, j: (i, 0, j)),
                   pl.BlockSpec((nseq_tile, 2, FF_COLS), lambda i, j: (i, 0, j))],
        out_shape=[jax.ShapeDtypeStruct((m, D_FF), BF16),
                   jax.ShapeDtypeStruct((m // seq_len, 2, D_FF), F32),
                   jax.ShapeDtypeStruct((m // seq_len, 2, D_FF), F32)],
        compiler_params=_params("parallel", "arbitrary"),
        name="ffn_up",
    )(h, w_up, w_up, conv_w, conv_w, conv_b, conv_b, prev, prev)


def _trunk(x3, conv_a_cache, swa_k_cache, swa_v_cache, ffn_cache, wts):
    bsz, seq_len, _ = x3.shape
    m = bsz * seq_len
    sample = ffn_cache is not None
    x = x3.reshape(m, D_MODEL)

    h = _rmsnorm(x, wts["norm_mix"][0], BF16)
    proj = _matmul([h], wts["w_in_ab"][0], tm=1024, tn=1024, name="in_ab")
    prev_a = conv_a_cache[0] if sample else jnp.zeros((bsz, 2, CONV_DIM), F32)
    a, conv_state = _conv_a(proj, wts["conv_w_a"][0], prev_a, seq_len)

    q_col = 3 * CONV_DIM // Q_DIM
    k_col = (3 * CONV_DIM + Q_DIM) // KV_DIM
    v_col = k_col + 1
    if sample:
        k_hist = swa_k_cache[0].reshape(bsz * WINDOW, KV_DIM)
        v_hist = swa_v_cache[0].reshape(bsz * WINDOW, KV_DIM)
        hist = [lambda b, c: (2 * b, 0), lambda b, c: (2 * b + 1, 0)] * 2
        new = [lambda b, c: (b, k_col), lambda b, c: (b, v_col)]
        attn = _attention(proj, lambda b, c: (b, q_col), k_hist, v_hist, hist, proj, proj, new,
                          wts["sinks"][0], mask_history=False, grid=(bsz, 1))
    else:
        nchunk = seq_len // CHUNK
        def blk(back, col):
            return lambda b, c: (b * nchunk + jnp.maximum(c - back, 0), col)
        hist = [blk(2, k_col), blk(1, k_col), blk(2, v_col), blk(1, v_col)]
        new = [blk(0, k_col), blk(0, v_col)]
        attn = _attention(proj, lambda b, c: (b * nchunk + c, q_col), proj, proj, hist, proj, proj, new,
                          wts["sinks"][0], mask_history=True, grid=(bsz, nchunk))

    k_new = proj[:, k_col * KV_DIM:(k_col + 1) * KV_DIM].reshape(bsz, seq_len, N_KV, HEAD_DIM)
    v_new = proj[:, v_col * KV_DIM:(v_col + 1) * KV_DIM].reshape(bsz, seq_len, N_KV, HEAD_DIM)
    if sample:
        k_state = jnp.concatenate([swa_k_cache[0], k_new], axis=1)[:, seq_len:]
        v_state = jnp.concatenate([swa_v_cache[0], v_new], axis=1)[:, seq_len:]
    else:
        k_state, v_state = k_new[:, seq_len - WINDOW:], v_new[:, seq_len - WINDOW:]

    x = _matmul([a, attn], wts["w_out_ab"][0], tm=1024, tn=1024, epilogue="residual", residual=x,
                name="out_ab")

    ffn_states = []

    def conv_ffn(x, layer):
        h = _rmsnorm(x, wts["norm_ffn"][layer], BF16)
        prev = ffn_cache[layer] if sample else jnp.zeros((bsz, 2, 2 * D_FF), F32)
        act, st_g, st_u = _ffn_up(h, wts["w_up_ffn"][layer], wts["conv_w_ffn"][layer],
                                  wts["conv_b_ffn"][layer], prev, seq_len)
        ffn_states.append(jnp.concatenate([st_g, st_u], axis=-1))
        return _matmul([act], wts["w_down_ffn"][layer], tm=1024, tn=256, epilogue="residual",
                       residual=x, name="ffn_down")

    x = conv_ffn(x, 0)

    h = _rmsnorm(x, wts["norm_mix"][1], BF16)
    z = _matmul([h], wts["w_in_c"][0], tm=1024, tn=1024, epilogue="gelu", name="in_c")
    cl = min(seq_len, GMLP_CHUNK)
    gated, vn = _gmlp_gate(z, wts["ln_g_c"][0], wts["ln_b_c"][0], wts["w_s_c"][0], wts["b_s_c"][0].T, cl)
    x = _matmul([gated], wts["w_out_c"][0], tm=1024, tn=1024, epilogue="residual", residual=x,
                name="out_c")
    x = conv_ffn(x, 1)

    y = _rmsnorm(x, wts["norm_final"], F32)
    return (y.reshape(bsz, seq_len, D_MODEL), conv_state[None], k_state[None], v_state[None],
            jnp.stack(ffn_states), vn.reshape(1, bsz, seq_len, D_MODEL))


def kernel(x_prompt, x_sample, state_conv_a, cache_swa_k, cache_swa_v, state_ffn_conv, norm_mix, norm_ffn,
           norm_final, w_in_ab, conv_w_a, sinks, w_out_ab, w_in_c, ln_g_c, ln_b_c, w_s_c, b_s_c, w_out_c,
           w_up_ffn, conv_w_ffn, conv_b_ffn, w_down_ffn):
    wts = dict(norm_mix=norm_mix, norm_ffn=norm_ffn, norm_final=norm_final,
               w_in_ab=w_in_ab.astype(BF16), conv_w_a=conv_w_a, sinks=sinks, w_out_ab=w_out_ab.astype(BF16),
               w_in_c=w_in_c.astype(BF16), ln_g_c=ln_g_c, ln_b_c=ln_b_c, w_s_c=w_s_c, b_s_c=b_s_c,
               w_out_c=w_out_c.astype(BF16), w_up_ffn=w_up_ffn.astype(BF16), conv_w_ffn=conv_w_ffn,
               conv_b_ffn=conv_b_ffn, w_down_ffn=w_down_ffn.astype(BF16))
    y_p, conv_p, k_p, v_p, ffn_p, _ = _trunk(x_prompt, None, None, None, None, wts)
    y_s, conv_s, k_s, v_s, ffn_s, gv_s = _trunk(x_sample, state_conv_a, cache_swa_k, cache_swa_v,
                                                state_ffn_conv, wts)
    return (y_p, y_s, conv_p, conv_s, k_p, k_s, v_p, v_s, ffn_p, ffn_s, gv_s)
```

```python
import functools

import numpy as np
import jax
import jax.numpy as jnp
from jax import lax
from jax.experimental import pallas as pl
from jax.experimental.pallas import tpu as pltpu

F32 = jnp.float32
BF16 = jnp.bfloat16

D_MODEL = 4096
CHUNK = 64
CONV_DIM = 2048
N_HEADS = 32
N_KV = 8
HEAD_DIM = 64
Q_DIM = N_HEADS * HEAD_DIM
KV_DIM = N_KV * HEAD_DIM
WINDOW = 128
IN_AB_DIM = 3 * CONV_DIM + Q_DIM + 2 * KV_DIM
GMLP_CHUNK = 128
GMLP_GROUPS = 16
GMLP_GDIM = D_MODEL // GMLP_GROUPS
D_FF = 11008
RMS_EPS = 1e-6
LN_EPS = 1e-5
NEG_INF = -1e30

V7X_VMEM_BYTES = 64 * 1024 * 1024
VMEM_LIMIT = V7X_VMEM_BYTES - 8 * 1024 * 1024

ROWS = 2048
SUB_ROWS = 512
FF_COLS = 256
N_FF_TILES = D_FF // FF_COLS
MM_F32W = dict(tm=2048, tn=512)
MM_DOWN = dict(tm=1024, tn=256)


def _params(*sem):
    return pltpu.CompilerParams(dimension_semantics=sem, vmem_limit_bytes=VMEM_LIMIT)


def _dot(a, b):
    return jnp.dot(a, b, preferred_element_type=F32)


def _as_bf16(x):
    return x if x.dtype == BF16 else x.astype(BF16)


def _rmsnorm_kernel(x_ref, g_ref, o_ref):
    x = x_ref[...]
    r = lax.rsqrt(jnp.mean(x * x, axis=-1, keepdims=True) + RMS_EPS)
    o_ref[...] = (x * r * g_ref[...]).astype(o_ref.dtype)


def _rmsnorm(x, g, out_dtype, tm=512):
    m, d = x.shape
    return pl.pallas_call(
        _rmsnorm_kernel,
        grid=(m // tm,),
        in_specs=[pl.BlockSpec((tm, d), lambda i: (i, 0)),
                  pl.BlockSpec((1, d), lambda i: (0, 0))],
        out_specs=pl.BlockSpec((tm, d), lambda i: (i, 0)),
        out_shape=jax.ShapeDtypeStruct((m, d), out_dtype),
        compiler_params=_params("parallel"),
        name="rmsnorm",
    )(x, g.reshape(1, d))


def _gelu(x):
    return 0.5 * x * (1.0 + lax.erf(x * np.float32(np.sqrt(0.5))))


def _mm_kernel(*refs, n_lhs, epilogue, tm):
    ws = [_as_bf16(refs[n_lhs + l][...]) for l in range(n_lhs)]
    rest = refs[2 * n_lhs:]
    o_ref = rest[-1]
    for r in range(tm // SUB_ROWS):
        rows = slice(r * SUB_ROWS, (r + 1) * SUB_ROWS)
        acc = _dot(refs[0][rows, :], ws[0])
        for l in range(1, n_lhs):
            acc = acc + _dot(refs[l][rows, :], ws[l])
        if epilogue == "residual":
            o_ref[rows, :] = rest[0][rows, :] + acc
        elif epilogue == "gelu":
            o_ref[rows, :] = _gelu(acc)
        else:
            o_ref[rows, :] = acc


def _matmul(lhs_list, w, layer, *, tm, tn, epilogue="none", residual=None, name):
    m = lhs_list[0].shape[0]
    n = w.shape[2]
    in_specs, args = [], []
    for l in lhs_list:
        in_specs.append(pl.BlockSpec((tm, l.shape[1]), lambda i, j: (i, 0),
                                     pipeline_mode=pl.Buffered(1)))
        args.append(l)
    row_off = 0
    for l in lhs_list:
        k = l.shape[1]
        in_specs.append(pl.BlockSpec((None, k, tn),
                                     functools.partial(lambda i, j, rb: (layer, rb, j), rb=row_off // k)))
        args.append(w)
        row_off += k
    if residual is not None:
        in_specs.append(pl.BlockSpec((tm, tn), lambda i, j: (i, j)))
        args.append(residual)
    return pl.pallas_call(
        functools.partial(_mm_kernel, n_lhs=len(lhs_list), epilogue=epilogue, tm=tm),
        grid=(m // tm, n // tn),
        in_specs=in_specs,
        out_specs=pl.BlockSpec((tm, tn), lambda i, j: (i, j)),
        out_shape=jax.ShapeDtypeStruct((m, n), F32),
        compiler_params=_params("parallel", "arbitrary"),
        name=name,
    )(*args)


def _causal_conv3(x, prev, w, seq_len):
    rows, c = x.shape
    nseq = rows // seq_len
    t = lax.broadcasted_iota(jnp.int32, (rows, 1), 0) & (seq_len - 1)
    p0 = jnp.broadcast_to(prev[:, 0:1, :], (nseq, seq_len, c)).reshape(rows, c)
    p1 = jnp.broadcast_to(prev[:, 1:2, :], (nseq, seq_len, c)).reshape(rows, c)
    x1 = jnp.where(t == 0, p1, pltpu.roll(x, 1, 0))
    x2 = jnp.where(t == 0, p0, jnp.where(t == 1, p1, pltpu.roll(x, 2, 0)))
    return x2 * w[0:1, :] + x1 * w[1:2, :] + x * w[2:3, :]


def _last_two_rows(x, seq_len):
    rows, c = x.shape
    return x.reshape(rows // seq_len, seq_len, c)[:, seq_len - 2:, :]


def _conv_a_kernel(gb_ref, gc_ref, hv_ref, w_ref, prev_ref, a_ref, st_ref, *, seq_len):
    xin = gc_ref[...] * hv_ref[...]
    conv = _causal_conv3(xin, prev_ref[...], w_ref[...], seq_len)
    a_ref[...] = (gb_ref[...] * conv).astype(a_ref.dtype)
    st_ref[...] = _last_two_rows(xin, seq_len)


def _conv_a(proj, conv_w, prev, seq_len, tc=256):
    m = proj.shape[0]
    nseq_tile = ROWS // seq_len
    nj = CONV_DIM // tc
    return pl.pallas_call(
        functools.partial(_conv_a_kernel, seq_len=seq_len),
        grid=(m // ROWS, nj),
        in_specs=[pl.BlockSpec((ROWS, tc), lambda i, j: (i, j)),
                  pl.BlockSpec((ROWS, tc), lambda i, j: (i, j + nj)),
                  pl.BlockSpec((ROWS, tc), lambda i, j: (i, j + 2 * nj)),
                  pl.BlockSpec((3, tc), lambda i, j: (0, j)),
                  pl.BlockSpec((nseq_tile, 2, tc), lambda i, j: (i, 0, j))],
        out_specs=[pl.BlockSpec((ROWS, tc), lambda i, j: (i, j)),
                   pl.BlockSpec((nseq_tile, 2, tc), lambda i, j: (i, 0, j))],
        out_shape=[jax.ShapeDtypeStruct((m, CONV_DIM), BF16),
                   jax.ShapeDtypeStruct((m // seq_len, 2, CONV_DIM), F32)],
        compiler_params=_params("parallel", "parallel"),
        name="conv_a",
    )(proj, proj, proj, conv_w, prev)


def _alibi_slopes():
    exps = np.arange(1, N_HEADS + 1, dtype=np.float32) * (8.0 / N_HEADS)
    return np.power(2.0, -exps).astype(np.float32)


def _attn_kernel(sink_ref, q_ref, kh_ref, kn_ref, vh_ref, vn_ref, o_ref, o_acc, *, tq, band_mask):
    nk = kh_ref.shape[0] + kn_ref.shape[0]
    group = N_HEADS // N_KV
    rows = group * tq
    t = lax.broadcasted_iota(jnp.int32, (rows, nk), 0) & (tq - 1)
    s = lax.broadcasted_iota(jnp.int32, (rows, nk), 1)
    neg_dist = -jnp.abs(t + (WINDOW - s)).astype(F32)
    if band_mask:
        q_chunk, k_chunk = t // CHUNK, s // CHUNK
        first_valid = jnp.where(pl.program_id(1) == 0, WINDOW // CHUNK, 0)
        visible = (k_chunk >= q_chunk) & (k_chunk <= q_chunk + WINDOW // CHUNK) & (k_chunk >= first_valid)
    head_in_group = lax.broadcasted_iota(jnp.int32, (rows, 1), 0) // tq
    slopes = _alibi_slopes()

    def per_head_column(vals):
        col = jnp.full((rows, 1), vals[group - 1], F32)
        for g in range(group - 2, -1, -1):
            col = jnp.where(head_in_group == g, vals[g], col)
        return col

    for kh in range(N_KV):
        cols = slice(kh * HEAD_DIM, (kh + 1) * HEAD_DIM)
        heads = range(kh * group, (kh + 1) * group)
        k_all = jnp.concatenate([kh_ref[:, cols], kn_ref[:, cols]], axis=0).astype(BF16)
        v_all = jnp.concatenate([vh_ref[:, cols], vn_ref[:, cols]], axis=0).astype(BF16)
        q = jnp.concatenate([q_ref[:, h * HEAD_DIM:(h + 1) * HEAD_DIM] for h in heads], axis=0).astype(BF16)
        slope = per_head_column([float(slopes[h]) for h in heads])
        sink = per_head_column([sink_ref[h] for h in heads])
        sc = lax.dot_general(q, k_all, (((1,), (1,)), ((), ())), preferred_element_type=F32)
        sc = sc * (HEAD_DIM ** -0.5) + slope * neg_dist
        if band_mask:
            sc = jnp.where(visible, sc, NEG_INF)
        m = jnp.maximum(jnp.max(sc, axis=-1, keepdims=True), sink)
        p = jnp.exp(sc - m)
        denom = jnp.sum(p, axis=-1, keepdims=True) + jnp.exp(sink - m)
        out = _dot((p * (1.0 / denom)).astype(BF16), v_all)
        for g, h in enumerate(heads):
            o_acc[:, h * HEAD_DIM:(h + 1) * HEAD_DIM] = out[g * tq:(g + 1) * tq, :]
    o_ref[...] = o_acc[...].astype(o_ref.dtype)


def _attention(q_src, q_map, k_hist, v_hist, hist_maps, k_new, v_new, new_maps, sinks, *, tq, band_mask, grid):
    m = q_src.shape[0]
    nblk = grid[1]
    return pl.pallas_call(
        functools.partial(_attn_kernel, tq=tq, band_mask=band_mask),
        grid=grid,
        in_specs=[pl.BlockSpec(memory_space=pltpu.SMEM),
                  pl.BlockSpec((tq, Q_DIM), q_map),
                  pl.BlockSpec((WINDOW, KV_DIM), hist_maps[0]), pl.BlockSpec((tq, KV_DIM), new_maps[0]),
                  pl.BlockSpec((WINDOW, KV_DIM), hist_maps[1]), pl.BlockSpec((tq, KV_DIM), new_maps[1])],
        out_specs=pl.BlockSpec((tq, Q_DIM), lambda b, c: (b * nblk + c, 0)),
        out_shape=jax.ShapeDtypeStruct((m, Q_DIM), BF16),
        scratch_shapes=[pltpu.VMEM((tq, Q_DIM), F32)],
        compiler_params=_params("parallel", "parallel"),
        name="swa",
    )(sinks, q_src, k_hist, k_new, v_hist, v_new)


def _gmlp_kernel(u_ref, v_ref, lng_ref, lnb_ref, ws_ref, bs_ref, o_ref, vn_ref, *, cl):
    v = v_ref[...]
    mu = jnp.mean(v, axis=-1, keepdims=True)
    var = jnp.mean(jnp.square(v - mu), axis=-1, keepdims=True)
    vn = (v - mu) * lax.rsqrt(var + LN_EPS) * lng_ref[...] + lnb_ref[...]
    vn_ref[...] = vn
    row = lax.broadcasted_iota(jnp.int32, (cl, cl), 0)
    col = lax.broadcasted_iota(jnp.int32, (cl, cl), 1)
    lower = row >= col
    for g in range(GMLP_GROUPS):
        cols = slice(g * GMLP_GDIM, (g + 1) * GMLP_GDIM)
        ws = jnp.where(lower, ws_ref[g, 0:cl, 0:cl], 0.0).astype(BF16)
        s = _dot(ws, vn[:, cols].astype(BF16)) + bs_ref[0:cl, g:g + 1]
        o_ref[:, cols] = (u_ref[:, cols] * s).astype(o_ref.dtype)


def _gmlp_gate(z, ln_g, ln_b, w_s, b_s_t, cl):
    m = z.shape[0]
    return pl.pallas_call(
        functools.partial(_gmlp_kernel, cl=cl),
        grid=(m // cl,),
        in_specs=[pl.BlockSpec((cl, D_MODEL), lambda i: (i, 0)),
                  pl.BlockSpec((cl, D_MODEL), lambda i: (i, 1)),
                  pl.BlockSpec((1, D_MODEL), lambda i: (0, 0)),
                  pl.BlockSpec((1, D_MODEL), lambda i: (0, 0)),
                  pl.BlockSpec((GMLP_GROUPS, GMLP_CHUNK, GMLP_CHUNK), lambda i: (0, 0, 0)),
                  pl.BlockSpec((GMLP_CHUNK, GMLP_GROUPS), lambda i: (0, 0))],
        out_specs=[pl.BlockSpec((cl, D_MODEL), lambda i: (i, 0)),
                   pl.BlockSpec((cl, D_MODEL), lambda i: (i, 0))],
        out_shape=[jax.ShapeDtypeStruct((m, D_MODEL), BF16),
                   jax.ShapeDtypeStruct((m, D_MODEL), F32)],
        compiler_params=_params("parallel"),
        name="gmlp_gate",
    )(z, z, ln_g.reshape(1, -1), ln_b.reshape(1, -1), w_s, b_s_t)


def _ffn_up_kernel(h_ref, wg_ref, wu_ref, cwg_ref, cwu_ref, bg_ref, bu_ref, pg_ref, pu_ref,
                   act_ref, sg_ref, su_ref, *, seq_len):
    sub_len = min(seq_len, SUB_ROWS)
    nseq_sub = SUB_ROWS // sub_len
    prev_g = prev_u = None
    wg, wu = _as_bf16(wg_ref[...]), _as_bf16(wu_ref[...])
    for r in range(ROWS // SUB_ROWS):
        rows = slice(r * SUB_ROWS, (r + 1) * SUB_ROWS)
        h = h_ref[rows, :]
        up_g = _dot(h, wg)
        up_u = _dot(h, wu)
        if seq_len >= ROWS and r > 0:
            pass
        else:
            seqs = slice(r * nseq_sub, (r + 1) * nseq_sub)
            prev_g, prev_u = pg_ref[seqs], pu_ref[seqs]
        cg = _causal_conv3(up_g, prev_g, cwg_ref[...], sub_len) + bg_ref[...]
        cu = _causal_conv3(up_u, prev_u, cwu_ref[...], sub_len) + bu_ref[...]
        act_ref[rows, :] = (cg * jax.nn.sigmoid(cg) * cu).astype(act_ref.dtype)
        prev_g, prev_u = _last_two_rows(up_g, sub_len), _last_two_rows(up_u, sub_len)
        if seq_len >= ROWS:
            if r == ROWS // SUB_ROWS - 1:
                sg_ref[...] = prev_g
                su_ref[...] = prev_u
        else:
            sg_ref[seqs] = prev_g
            su_ref[seqs] = prev_u


def _ffn_up(h, w_up, conv_w, conv_b, layer, prev, prev_layer, seq_len):
    m = h.shape[0]
    nseq_tile = ROWS // seq_len
    nj = N_FF_TILES
    st_spec_g = pl.BlockSpec((None, nseq_tile, 2, FF_COLS), lambda i, j: (prev_layer, i, 0, j))
    st_spec_u = pl.BlockSpec((None, nseq_tile, 2, FF_COLS), lambda i, j: (prev_layer, i, 0, j + nj))
    conv_b = conv_b.reshape(conv_b.shape[0], 1, -1)
    return pl.pallas_call(
        functools.partial(_ffn_up_kernel, seq_len=seq_len),
        grid=(m // ROWS, nj),
        in_specs=[pl.BlockSpec((ROWS, D_MODEL), lambda i, j: (i, 0), pipeline_mode=pl.Buffered(1)),
                  pl.BlockSpec((None, D_MODEL, FF_COLS), lambda i, j: (layer, 0, j)),
                  pl.BlockSpec((None, D_MODEL, FF_COLS), lambda i, j: (layer, 0, j + nj)),
                  pl.BlockSpec((None, 3, FF_COLS), lambda i, j: (layer, 0, j)),
                  pl.BlockSpec((None, 3, FF_COLS), lambda i, j: (layer, 0, j + nj)),
                  pl.BlockSpec((None, 1, FF_COLS), lambda i, j: (layer, 0, j)),
                  pl.BlockSpec((None, 1, FF_COLS), lambda i, j: (layer, 0, j + nj)),
                  st_spec_g, st_spec_u],
        out_specs=[pl.BlockSpec((ROWS, FF_COLS), lambda i, j: (i, j)),
                   pl.BlockSpec((nseq_tile, 2, FF_COLS), lambda i, j: (i, 0, j)),
                   pl.BlockSpec((nseq_tile, 2, FF_COLS), lambda i, j: (i, 0, j))],
        out_shape=[jax.ShapeDtypeStruct((m, D_FF), BF16),
                   jax.ShapeDtypeStruct((m // seq_len, 2, D_FF), F32),
                   jax.ShapeDtypeStruct((m // seq_len, 2, D_FF), F32)],
        compiler_params=_params("parallel", "arbitrary"),
        name="ffn_up",
    )(h, w_up, w_up, conv_w, conv_w, conv_b, conv_b, prev, prev)


def _trunk(x3, conv_a_cache, swa_k_cache, swa_v_cache, ffn_cache, wts):
    bsz, seq_len, _ = x3.shape
    m = bsz * seq_len
    sample = ffn_cache is not None
    x = x3.reshape(m, D_MODEL)

    h = _rmsnorm(x, wts["norm_mix"][0], BF16)
    proj = _matmul([h], wts["w_in_ab"], 0, **MM_F32W, name="in_ab")
    prev_a = conv_a_cache[0] if sample else jnp.zeros((bsz, 2, CONV_DIM), F32)
    a, conv_state = _conv_a(proj, wts["conv_w_a"][0], prev_a, seq_len)

    q_col = 3 * CONV_DIM // Q_DIM
    k_col = (3 * CONV_DIM + Q_DIM) // KV_DIM
    v_col = k_col + 1
    if sample:
        k_hist = swa_k_cache[0].reshape(bsz * WINDOW, KV_DIM)
        v_hist = swa_v_cache[0].reshape(bsz * WINDOW, KV_DIM)
        hist = [lambda b, c: (b, 0)] * 2
        new = [lambda b, c: (b, k_col), lambda b, c: (b, v_col)]
        attn = _attention(proj, lambda b, c: (b, q_col), k_hist, v_hist, hist, proj, proj, new,
                          wts["sinks"][0], tq=seq_len, band_mask=False, grid=(bsz, 1))
    else:
        nblk = seq_len // WINDOW
        def blk(back, col):
            return lambda b, c: (b * nblk + jnp.maximum(c - back, 0), col)
        attn = _attention(proj, blk(0, q_col), proj, proj, [blk(1, k_col), blk(1, v_col)], proj, proj,
                          [blk(0, k_col), blk(0, v_col)], wts["sinks"][0], tq=WINDOW, band_mask=True,
                          grid=(bsz, nblk))

    k_new = proj[:, k_col * KV_DIM:(k_col + 1) * KV_DIM].reshape(bsz, seq_len, N_KV, HEAD_DIM)
    v_new = proj[:, v_col * KV_DIM:(v_col + 1) * KV_DIM].reshape(bsz, seq_len, N_KV, HEAD_DIM)
    if sample:
        k_state = jnp.concatenate([swa_k_cache[0], k_new], axis=1)[:, seq_len:]
        v_state = jnp.concatenate([swa_v_cache[0], v_new], axis=1)[:, seq_len:]
    else:
        k_state, v_state = k_new[:, seq_len - WINDOW:], v_new[:, seq_len - WINDOW:]

    x = _matmul([a, attn], wts["w_out_ab"], 0, **MM_F32W, epilogue="residual", residual=x,
                name="out_ab")

    ffn_states = []

    def conv_ffn(x, layer):
        h = _rmsnorm(x, wts["norm_ffn"][layer], BF16)
        prev, prev_layer = (ffn_cache, layer) if sample else (jnp.zeros((1, bsz, 2, 2 * D_FF), F32), 0)
        act, st_g, st_u = _ffn_up(h, wts["w_up_ffn"], wts["conv_w_ffn"], wts["conv_b_ffn"], layer,
                                  prev, prev_layer, seq_len)
        ffn_states.append(jnp.concatenate([st_g, st_u], axis=-1))
        return _matmul([act], wts["w_down_ffn"], layer, **MM_DOWN, epilogue="residual",
                       residual=x, name="ffn_down")

    x = conv_ffn(x, 0)

    h = _rmsnorm(x, wts["norm_mix"][1], BF16)
    z = _matmul([h], wts["w_in_c"], 0, **MM_F32W, epilogue="gelu", name="in_c")
    cl = min(seq_len, GMLP_CHUNK)
    gated, vn = _gmlp_gate(z, wts["ln_g_c"][0], wts["ln_b_c"][0], wts["w_s_c"][0], wts["b_s_c"][0].T, cl)
    x = _matmul([gated], wts["w_out_c"], 0, **MM_F32W, epilogue="residual", residual=x,
                name="out_c")
    x = conv_ffn(x, 1)

    y = _rmsnorm(x, wts["norm_final"], F32)
    return (y.reshape(bsz, seq_len, D_MODEL), conv_state[None], k_state[None], v_state[None],
            jnp.stack(ffn_states), vn.reshape(1, bsz, seq_len, D_MODEL))


def kernel(x_prompt, x_sample, state_conv_a, cache_swa_k, cache_swa_v, state_ffn_conv, norm_mix, norm_ffn,
           norm_final, w_in_ab, conv_w_a, sinks, w_out_ab, w_in_c, ln_g_c, ln_b_c, w_s_c, b_s_c, w_out_c,
           w_up_ffn, conv_w_ffn, conv_b_ffn, w_down_ffn):
    wts = dict(norm_mix=norm_mix, norm_ffn=norm_ffn, norm_final=norm_final,
               w_in_ab=w_in_ab, conv_w_a=conv_w_a, sinks=sinks, w_out_ab=w_out_ab,
               w_in_c=w_in_c, ln_g_c=ln_g_c, ln_b_c=ln_b_c, w_s_c=w_s_c, b_s_c=b_s_c,
               w_out_c=w_out_c, w_up_ffn=w_up_ffn, conv_w_ffn=conv_w_ffn,
               conv_b_ffn=conv_b_ffn, w_down_ffn=w_down_ffn.astype(BF16))
    y_p, conv_p, k_p, v_p, ffn_p, _ = _trunk(x_prompt, None, None, None, None, wts)
    y_s, conv_s, k_s, v_s, ffn_s, gv_s = _trunk(x_sample, state_conv_a, cache_swa_k, cache_swa_v,
                                                state_ffn_conv, wts)
    return (y_p, y_s, conv_p, conv_s, k_p, k_s, v_p, v_s, ffn_p, ffn_s, gv_s)
```

```python
import functools

import numpy as np
import jax
import jax.numpy as jnp
from jax import lax
from jax.experimental import pallas as pl
from jax.experimental.pallas import tpu as pltpu

F32 = jnp.float32
BF16 = jnp.bfloat16

D_MODEL = 4096
CHUNK = 64
CONV_DIM = 2048
N_HEADS = 32
N_KV = 8
HEAD_DIM = 64
Q_DIM = N_HEADS * HEAD_DIM
KV_DIM = N_KV * HEAD_DIM
WINDOW = 128
IN_AB_DIM = 3 * CONV_DIM + Q_DIM + 2 * KV_DIM
GMLP_CHUNK = 128
GMLP_GROUPS = 16
GMLP_GDIM = D_MODEL // GMLP_GROUPS
D_FF = 11008
RMS_EPS = 1e-6
LN_EPS = 1e-5
NEG_INF = -1e30

V7X_VMEM_BYTES = 64 * 1024 * 1024
VMEM_LIMIT = V7X_VMEM_BYTES - 8 * 1024 * 1024

ROWS = 2048
SUB_ROWS = 512
EP_ROWS = 32
HALO = 8
FF_COLS = 256
N_FF_TILES = D_FF // FF_COLS
LANES = 128
N_SLAB = FF_COLS // LANES
SCR_PITCH = 2
MM_F32W = dict(tm=2048, tn=512)
MM_DOWN = dict(tm=1024, tn=256)


def _params(*sem):
    return pltpu.CompilerParams(dimension_semantics=sem, vmem_limit_bytes=VMEM_LIMIT)


def _dot(a, b):
    return jnp.dot(a, b, preferred_element_type=F32)


def _as_bf16(x):
    return x if x.dtype == BF16 else x.astype(BF16)


def _rmsnorm_kernel(x_ref, g_ref, o_ref):
    x = x_ref[...]
    r = lax.rsqrt(jnp.mean(x * x, axis=-1, keepdims=True) + RMS_EPS)
    o_ref[...] = (x * r * g_ref[...]).astype(o_ref.dtype)


def _rmsnorm(x, g, out_dtype, tm=512):
    m, d = x.shape
    return pl.pallas_call(
        _rmsnorm_kernel,
        grid=(m // tm,),
        in_specs=[pl.BlockSpec((tm, d), lambda i: (i, 0)),
                  pl.BlockSpec((1, d), lambda i: (0, 0))],
        out_specs=pl.BlockSpec((tm, d), lambda i: (i, 0)),
        out_shape=jax.ShapeDtypeStruct((m, d), out_dtype),
        compiler_params=_params("parallel"),
        name="rmsnorm",
    )(x, g.reshape(1, d))


def _gelu(x):
    return 0.5 * x * (1.0 + lax.erf(x * np.float32(np.sqrt(0.5))))


def _mm_kernel(*refs, n_lhs, epilogue, tm):
    ws = [_as_bf16(refs[n_lhs + l][...]) for l in range(n_lhs)]
    rest = refs[2 * n_lhs:]
    o_ref = rest[-1]
    for r in range(tm // SUB_ROWS):
        rows = slice(r * SUB_ROWS, (r + 1) * SUB_ROWS)
        acc = _dot(refs[0][rows, :], ws[0])
        for l in range(1, n_lhs):
            acc = acc + _dot(refs[l][rows, :], ws[l])
        if epilogue == "residual":
            o_ref[rows, :] = rest[0][rows, :] + acc
        elif epilogue == "gelu":
            o_ref[rows, :] = _gelu(acc)
        else:
            o_ref[rows, :] = acc


def _matmul(lhs_list, w, layer, *, tm, tn, epilogue="none", residual=None, name):
    m = lhs_list[0].shape[0]
    n = w.shape[2]
    in_specs, args = [], []
    for l in lhs_list:
        in_specs.append(pl.BlockSpec((tm, l.shape[1]), lambda i, j: (i, 0),
                                     pipeline_mode=pl.Buffered(1)))
        args.append(l)
    row_off = 0
    for l in lhs_list:
        k = l.shape[1]
        in_specs.append(pl.BlockSpec((None, k, tn),
                                     functools.partial(lambda i, j, rb: (layer, rb, j), rb=row_off // k)))
        args.append(w)
        row_off += k
    if residual is not None:
        in_specs.append(pl.BlockSpec((tm, tn), lambda i, j: (i, j)))
        args.append(residual)
    return pl.pallas_call(
        functools.partial(_mm_kernel, n_lhs=len(lhs_list), epilogue=epilogue, tm=tm),
        grid=(m // tm, n // tn),
        in_specs=in_specs,
        out_specs=pl.BlockSpec((tm, tn), lambda i, j: (i, j)),
        out_shape=jax.ShapeDtypeStruct((m, n), F32),
        compiler_params=_params("parallel", "arbitrary"),
        name=name,
    )(*args)


def _causal_conv3(x, prev, w, seq_len):
    rows, c = x.shape
    nseq = rows // seq_len
    t = lax.broadcasted_iota(jnp.int32, (rows, 1), 0) & (seq_len - 1)
    p0 = jnp.broadcast_to(prev[:, 0:1, :], (nseq, seq_len, c)).reshape(rows, c)
    p1 = jnp.broadcast_to(prev[:, 1:2, :], (nseq, seq_len, c)).reshape(rows, c)
    x1 = jnp.where(t == 0, p1, pltpu.roll(x, 1, 0))
    x2 = jnp.where(t == 0, p0, jnp.where(t == 1, p1, pltpu.roll(x, 2, 0)))
    return x2 * w[0:1, :] + x1 * w[1:2, :] + x * w[2:3, :]


def _last_two_rows(x, seq_len):
    rows, c = x.shape
    return x.reshape(rows // seq_len, seq_len, c)[:, seq_len - 2:, :]


def _conv_a_kernel(gb_ref, gc_ref, hv_ref, w_ref, prev_ref, a_ref, st_ref, *, seq_len):
    xin = gc_ref[...] * hv_ref[...]
    conv = _causal_conv3(xin, prev_ref[...], w_ref[...], seq_len)
    a_ref[...] = (gb_ref[...] * conv).astype(a_ref.dtype)
    st_ref[...] = _last_two_rows(xin, seq_len)


def _conv_a(proj, conv_w, prev, seq_len, tc=256):
    m = proj.shape[0]
    nseq_tile = ROWS // seq_len
    nj = CONV_DIM // tc
    return pl.pallas_call(
        functools.partial(_conv_a_kernel, seq_len=seq_len),
        grid=(m // ROWS, nj),
        in_specs=[pl.BlockSpec((ROWS, tc), lambda i, j: (i, j)),
                  pl.BlockSpec((ROWS, tc), lambda i, j: (i, j + nj)),
                  pl.BlockSpec((ROWS, tc), lambda i, j: (i, j + 2 * nj)),
                  pl.BlockSpec((3, tc), lambda i, j: (0, j)),
                  pl.BlockSpec((nseq_tile, 2, tc), lambda i, j: (i, 0, j))],
        out_specs=[pl.BlockSpec((ROWS, tc), lambda i, j: (i, j)),
                   pl.BlockSpec((nseq_tile, 2, tc), lambda i, j: (i, 0, j))],
        out_shape=[jax.ShapeDtypeStruct((m, CONV_DIM), BF16),
                   jax.ShapeDtypeStruct((m // seq_len, 2, CONV_DIM), F32)],
        compiler_params=_params("parallel", "parallel"),
        name="conv_a",
    )(proj, proj, proj, conv_w, prev)


def _alibi_slopes():
    exps = np.arange(1, N_HEADS + 1, dtype=np.float32) * (8.0 / N_HEADS)
    return np.power(2.0, -exps).astype(np.float32)


def _attn_kernel(sink_ref, q_ref, kh_ref, kn_ref, vh_ref, vn_ref, o_ref, o_acc, *, tq, band_mask):
    nk = kh_ref.shape[0] + kn_ref.shape[0]
    group = N_HEADS // N_KV
    rows = group * tq
    t = lax.broadcasted_iota(jnp.int32, (rows, nk), 0) & (tq - 1)
    s = lax.broadcasted_iota(jnp.int32, (rows, nk), 1)
    neg_dist = -jnp.abs(t + (WINDOW - s)).astype(F32)
    if band_mask:
        q_chunk, k_chunk = t // CHUNK, s // CHUNK
        first_valid = jnp.where(pl.program_id(1) == 0, WINDOW // CHUNK, 0)
        visible = (k_chunk >= q_chunk) & (k_chunk <= q_chunk + WINDOW // CHUNK) & (k_chunk >= first_valid)
    head_in_group = lax.broadcasted_iota(jnp.int32, (rows, 1), 0) // tq
    slopes = _alibi_slopes()

    def per_head_column(vals):
        col = jnp.full((rows, 1), vals[group - 1], F32)
        for g in range(group - 2, -1, -1):
            col = jnp.where(head_in_group == g, vals[g], col)
        return col

    for kh in range(N_KV):
        cols = slice(kh * HEAD_DIM, (kh + 1) * HEAD_DIM)
        heads = range(kh * group, (kh + 1) * group)
        k_all = jnp.concatenate([kh_ref[:, cols], kn_ref[:, cols]], axis=0).astype(BF16)
        v_all = jnp.concatenate([vh_ref[:, cols], vn_ref[:, cols]], axis=0).astype(BF16)
        q = jnp.concatenate([q_ref[:, h * HEAD_DIM:(h + 1) * HEAD_DIM] for h in heads], axis=0).astype(BF16)
        slope = per_head_column([float(slopes[h]) for h in heads])
        sink = per_head_column([sink_ref[h] for h in heads])
        sc = lax.dot_general(q, k_all, (((1,), (1,)), ((), ())), preferred_element_type=F32)
        sc = sc * (HEAD_DIM ** -0.5) + slope * neg_dist
        if band_mask:
            sc = jnp.where(visible, sc, NEG_INF)
        m = jnp.maximum(jnp.max(sc, axis=-1, keepdims=True), sink)
        p = jnp.exp(sc - m)
        denom = jnp.sum(p, axis=-1, keepdims=True) + jnp.exp(sink - m)
        out = _dot((p * (1.0 / denom)).astype(BF16), v_all)
        for g, h in enumerate(heads):
            o_acc[:, h * HEAD_DIM:(h + 1) * HEAD_DIM] = out[g * tq:(g + 1) * tq, :]
    o_ref[...] = o_acc[...].astype(o_ref.dtype)


def _attention(q_src, q_map, k_hist, v_hist, hist_maps, k_new, v_new, new_maps, sinks, *, tq, band_mask, grid):
    m = q_src.shape[0]
    nblk = grid[1]
    return pl.pallas_call(
        functools.partial(_attn_kernel, tq=tq, band_mask=band_mask),
        grid=grid,
        in_specs=[pl.BlockSpec(memory_space=pltpu.SMEM),
                  pl.BlockSpec((tq, Q_DIM), q_map),
                  pl.BlockSpec((WINDOW, KV_DIM), hist_maps[0]), pl.BlockSpec((tq, KV_DIM), new_maps[0]),
                  pl.BlockSpec((WINDOW, KV_DIM), hist_maps[1]), pl.BlockSpec((tq, KV_DIM), new_maps[1])],
        out_specs=pl.BlockSpec((tq, Q_DIM), lambda b, c: (b * nblk + c, 0)),
        out_shape=jax.ShapeDtypeStruct((m, Q_DIM), BF16),
        scratch_shapes=[pltpu.VMEM((tq, Q_DIM), F32)],
        compiler_params=_params("parallel", "parallel"),
        name="swa",
    )(sinks, q_src, k_hist, k_new, v_hist, v_new)


def _gmlp_kernel(u_ref, v_ref, lng_ref, lnb_ref, ws_ref, bs_ref, o_ref, vn_ref, *, cl):
    v = v_ref[...]
    mu = jnp.mean(v, axis=-1, keepdims=True)
    var = jnp.mean(jnp.square(v - mu), axis=-1, keepdims=True)
    vn = (v - mu) * lax.rsqrt(var + LN_EPS) * lng_ref[...] + lnb_ref[...]
    vn_ref[...] = vn
    row = lax.broadcasted_iota(jnp.int32, (cl, cl), 0)
    col = lax.broadcasted_iota(jnp.int32, (cl, cl), 1)
    lower = row >= col
    for g in range(GMLP_GROUPS):
        cols = slice(g * GMLP_GDIM, (g + 1) * GMLP_GDIM)
        ws = jnp.where(lower, ws_ref[g, 0:cl, 0:cl], 0.0).astype(BF16)
        s = _dot(ws, vn[:, cols].astype(BF16)) + bs_ref[0:cl, g:g + 1]
        o_ref[:, cols] = (u_ref[:, cols] * s).astype(o_ref.dtype)


def _gmlp_gate(z, ln_g, ln_b, w_s, b_s_t, cl):
    m = z.shape[0]
    return pl.pallas_call(
        functools.partial(_gmlp_kernel, cl=cl),
        grid=(m // cl,),
        in_specs=[pl.BlockSpec((cl, D_MODEL), lambda i: (i, 0)),
                  pl.BlockSpec((cl, D_MODEL), lambda i: (i, 1)),
                  pl.BlockSpec((1, D_MODEL), lambda i: (0, 0)),
                  pl.BlockSpec((1, D_MODEL), lambda i: (0, 0)),
                  pl.BlockSpec((GMLP_GROUPS, GMLP_CHUNK, GMLP_CHUNK), lambda i: (0, 0, 0)),
                  pl.BlockSpec((GMLP_CHUNK, GMLP_GROUPS), lambda i: (0, 0))],
        out_specs=[pl.BlockSpec((cl, D_MODEL), lambda i: (i, 0)),
                   pl.BlockSpec((cl, D_MODEL), lambda i: (i, 0))],
        out_shape=[jax.ShapeDtypeStruct((m, D_MODEL), BF16),
                   jax.ShapeDtypeStruct((m, D_MODEL), F32)],
        compiler_params=_params("parallel"),
        name="gmlp_gate",
    )(z, z, ln_g.reshape(1, -1), ln_b.reshape(1, -1), w_s, b_s_t)


def _ffn_up_kernel(h_ref, wg_ref, wu_ref, cwg_ref, cwu_ref, bg_ref, bu_ref, pg_ref, pu_ref,
                   act_ref, sg_ref, su_ref, ug_scr, uu_scr, *, seq_len):
    wg, wu = _as_bf16(wg_ref[...]), _as_bf16(wu_ref[...])
    ug_scr[:, 0:SCR_PITCH * HALO, :] = jnp.zeros((N_SLAB, SCR_PITCH * HALO, LANES), F32)
    uu_scr[:, 0:SCR_PITCH * HALO, :] = jnp.zeros((N_SLAB, SCR_PITCH * HALO, LANES), F32)
    row = lax.broadcasted_iota(jnp.int32, (EP_ROWS, 1), 0)

    def rows_at(scr, s, t, n):
        return scr[s, pl.ds(SCR_PITCH * (HALO + t), n, stride=SCR_PITCH), :]

    def conv(scr, prev_ref, cw_ref, b_ref, s, start):
        lanes = slice(s * LANES, (s + 1) * LANES)
        x0, x1, x2 = (rows_at(scr, s, start - back, EP_ROWS) for back in range(3))
        if start % seq_len == 0:
            prev = prev_ref[start // seq_len]
            x1 = jnp.where(row == 0, prev[1:2, lanes], x1)
            x2 = jnp.where(row == 0, prev[0:1, lanes], jnp.where(row == 1, prev[1:2, lanes], x2))
        return x2 * cw_ref[0:1, lanes] + x1 * cw_ref[1:2, lanes] + x0 * cw_ref[2:3, lanes] + b_ref[:, lanes]

    for r in range(ROWS // SUB_ROWS):
        h = h_ref[r * SUB_ROWS:(r + 1) * SUB_ROWS, :]
        up_g, up_u = _dot(h, wg), _dot(h, wu)
        for s in range(N_SLAB):
            dst = pl.ds(SCR_PITCH * (HALO + r * SUB_ROWS), SUB_ROWS, stride=SCR_PITCH)
            ug_scr[s, dst, :] = up_g[:, s * LANES:(s + 1) * LANES]
            uu_scr[s, dst, :] = up_u[:, s * LANES:(s + 1) * LANES]
        for c in range(SUB_ROWS // EP_ROWS):
            start = r * SUB_ROWS + c * EP_ROWS
            end = start + EP_ROWS
            for s in range(N_SLAB):
                lanes = slice(s * LANES, (s + 1) * LANES)
                cg = conv(ug_scr, pg_ref, cwg_ref, bg_ref, s, start)
                cu = conv(uu_scr, pu_ref, cwu_ref, bu_ref, s, start)
                act_ref[start:end, lanes] = (cg * jax.nn.sigmoid(cg) * cu).astype(act_ref.dtype)
                if end % seq_len == 0:
                    sg_ref[end // seq_len - 1, :, lanes] = rows_at(ug_scr, s, end - 2, 2)
                    su_ref[end // seq_len - 1, :, lanes] = rows_at(uu_scr, s, end - 2, 2)


def _ffn_up(h, w_up, conv_w, conv_b, layer, prev, prev_layer, seq_len):
    m = h.shape[0]
    nseq_tile = ROWS // seq_len
    nj = N_FF_TILES
    st_spec_g = pl.BlockSpec((None, nseq_tile, 2, FF_COLS), lambda i, j: (prev_layer, i, 0, j))
    st_spec_u = pl.BlockSpec((None, nseq_tile, 2, FF_COLS), lambda i, j: (prev_layer, i, 0, j + nj))
    conv_b = conv_b.reshape(conv_b.shape[0], 1, -1)
    return pl.pallas_call(
        functools.partial(_ffn_up_kernel, seq_len=seq_len),
        grid=(m // ROWS, nj),
        in_specs=[pl.BlockSpec((ROWS, D_MODEL), lambda i, j: (i, 0), pipeline_mode=pl.Buffered(1)),
                  pl.BlockSpec((None, D_MODEL, FF_COLS), lambda i, j: (layer, 0, j)),
                  pl.BlockSpec((None, D_MODEL, FF_COLS), lambda i, j: (layer, 0, j + nj)),
                  pl.BlockSpec((None, 3, FF_COLS), lambda i, j: (layer, 0, j)),
                  pl.BlockSpec((None, 3, FF_COLS), lambda i, j: (layer, 0, j + nj)),
                  pl.BlockSpec((None, 1, FF_COLS), lambda i, j: (layer, 0, j)),
                  pl.BlockSpec((None, 1, FF_COLS), lambda i, j: (layer, 0, j + nj)),
                  st_spec_g, st_spec_u],
        out_specs=[pl.BlockSpec((ROWS, FF_COLS), lambda i, j: (i, j)),
                   pl.BlockSpec((nseq_tile, 2, FF_COLS), lambda i, j: (i, 0, j)),
                   pl.BlockSpec((nseq_tile, 2, FF_COLS), lambda i, j: (i, 0, j))],
        out_shape=[jax.ShapeDtypeStruct((m, D_FF), BF16),
                   jax.ShapeDtypeStruct((m // seq_len, 2, D_FF), F32),
                   jax.ShapeDtypeStruct((m // seq_len, 2, D_FF), F32)],
        scratch_shapes=[pltpu.VMEM((N_SLAB, SCR_PITCH * (HALO + ROWS), LANES), F32)] * 2,
        compiler_params=_params("parallel", "arbitrary"),
        name="ffn_up",
    )(h, w_up, w_up, conv_w, conv_w, conv_b, conv_b, prev, prev)


def _trunk(x3, conv_a_cache, swa_k_cache, swa_v_cache, ffn_cache, wts):
    bsz, seq_len, _ = x3.shape
    m = bsz * seq_len
    sample = ffn_cache is not None
    x = x3.reshape(m, D_MODEL)

    h = _rmsnorm(x, wts["norm_mix"][0], BF16)
    proj = _matmul([h], wts["w_in_ab"], 0, **MM_F32W, name="in_ab")
    prev_a = conv_a_cache[0] if sample else jnp.zeros((bsz, 2, CONV_DIM), F32)
    a, conv_state = _conv_a(proj, wts["conv_w_a"][0], prev_a, seq_len)

    q_col = 3 * CONV_DIM // Q_DIM
    k_col = (3 * CONV_DIM + Q_DIM) // KV_DIM
    v_col = k_col + 1
    if sample:
        k_hist = swa_k_cache[0].reshape(bsz * WINDOW, KV_DIM)
        v_hist = swa_v_cache[0].reshape(bsz * WINDOW, KV_DIM)
        hist = [lambda b, c: (b, 0)] * 2
        new = [lambda b, c: (b, k_col), lambda b, c: (b, v_col)]
        attn = _attention(proj, lambda b, c: (b, q_col), k_hist, v_hist, hist, proj, proj, new,
                          wts["sinks"][0], tq=seq_len, band_mask=False, grid=(bsz, 1))
    else:
        nblk = seq_len // WINDOW
        def blk(back, col):
            return lambda b, c: (b * nblk + jnp.maximum(c - back, 0), col)
        attn = _attention(proj, blk(0, q_col), proj, proj, [blk(1, k_col), blk(1, v_col)], proj, proj,
                          [blk(0, k_col), blk(0, v_col)], wts["sinks"][0], tq=WINDOW, band_mask=True,
                          grid=(bsz, nblk))

    k_new = proj[:, k_col * KV_DIM:(k_col + 1) * KV_DIM].reshape(bsz, seq_len, N_KV, HEAD_DIM)
    v_new = proj[:, v_col * KV_DIM:(v_col + 1) * KV_DIM].reshape(bsz, seq_len, N_KV, HEAD_DIM)
    if sample:
        k_state = jnp.concatenate([swa_k_cache[0], k_new], axis=1)[:, seq_len:]
        v_state = jnp.concatenate([swa_v_cache[0], v_new], axis=1)[:, seq_len:]
    else:
        k_state, v_state = k_new[:, seq_len - WINDOW:], v_new[:, seq_len - WINDOW:]

    x = _matmul([a, attn], wts["w_out_ab"], 0, **MM_F32W, epilogue="residual", residual=x,
                name="out_ab")

    ffn_states = []

    def conv_ffn(x, layer):
        h = _rmsnorm(x, wts["norm_ffn"][layer], BF16)
        prev, prev_layer = (ffn_cache, layer) if sample else (jnp.zeros((1, bsz, 2, 2 * D_FF), F32), 0)
        act, st_g, st_u = _ffn_up(h, wts["w_up_ffn"], wts["conv_w_ffn"], wts["conv_b_ffn"], layer,
                                  prev, prev_layer, seq_len)
        ffn_states.append(jnp.concatenate([st_g, st_u], axis=-1))
        return _matmul([act], wts["w_down_ffn"], layer, **MM_DOWN, epilogue="residual",
                       residual=x, name="ffn_down")

    x = conv_ffn(x, 0)

    h = _rmsnorm(x, wts["norm_mix"][1], BF16)
    z = _matmul([h], wts["w_in_c"], 0, **MM_F32W, epilogue="gelu", name="in_c")
    cl = min(seq_len, GMLP_CHUNK)
    gated, vn = _gmlp_gate(z, wts["ln_g_c"][0], wts["ln_b_c"][0], wts["w_s_c"][0], wts["b_s_c"][0].T, cl)
    x = _matmul([gated], wts["w_out_c"], 0, **MM_F32W, epilogue="residual", residual=x,
                name="out_c")
    x = conv_ffn(x, 1)

    y = _rmsnorm(x, wts["norm_final"], F32)
    return (y.reshape(bsz, seq_len, D_MODEL), conv_state[None], k_state[None], v_state[None],
            jnp.stack(ffn_states), vn.reshape(1, bsz, seq_len, D_MODEL))


def kernel(x_prompt, x_sample, state_conv_a, cache_swa_k, cache_swa_v, state_ffn_conv, norm_mix, norm_ffn,
           norm_final, w_in_ab, conv_w_a, sinks, w_out_ab, w_in_c, ln_g_c, ln_b_c, w_s_c, b_s_c, w_out_c,
           w_up_ffn, conv_w_ffn, conv_b_ffn, w_down_ffn):
    wts = dict(norm_mix=norm_mix, norm_ffn=norm_ffn, norm_final=norm_final,
               w_in_ab=w_in_ab, conv_w_a=conv_w_a, sinks=sinks, w_out_ab=w_out_ab,
               w_in_c=w_in_c, ln_g_c=ln_g_c, ln_b_c=ln_b_c, w_s_c=w_s_c, b_s_c=b_s_c,
               w_out_c=w_out_c, w_up_ffn=w_up_ffn, conv_w_ffn=conv_w_ffn,
               conv_b_ffn=conv_b_ffn, w_down_ffn=w_down_ffn.astype(BF16))
    y_p, conv_p, k_p, v_p, ffn_p, _ = _trunk(x_prompt, None, None, None, None, wts)
    y_s, conv_s, k_s, v_s, ffn_s, gv_s = _trunk(x_sample, state_conv_a, cache_swa_k, cache_swa_v,
                                                state_ffn_conv, wts)
    return (y_p, y_s, conv_p, conv_s, k_p, k_s, v_p, v_s, ffn_p, ffn_s, gv_s)
```

```python
import functools

import numpy as np
import jax
import jax.numpy as jnp
from jax import lax
from jax.experimental import pallas as pl
from jax.experimental.pallas import tpu as pltpu

F32 = jnp.float32
BF16 = jnp.bfloat16

D_MODEL = 4096
CHUNK = 64
CONV_DIM = 2048
N_HEADS = 32
N_KV = 8
HEAD_DIM = 64
Q_DIM = N_HEADS * HEAD_DIM
KV_DIM = N_KV * HEAD_DIM
WINDOW = 128
IN_AB_DIM = 3 * CONV_DIM + Q_DIM + 2 * KV_DIM
GMLP_CHUNK = 128
GMLP_GROUPS = 16
GMLP_GDIM = D_MODEL // GMLP_GROUPS
D_FF = 11008
RMS_EPS = 1e-6
LN_EPS = 1e-5
NEG_INF = -1e30

V7X_VMEM_BYTES = 64 * 1024 * 1024
VMEM_LIMIT = V7X_VMEM_BYTES - 8 * 1024 * 1024

ROWS = 2048
SUB_ROWS = 512
EP_ROWS = 32
FFN_SUB_TILES = (512, 512, 512, 320, 192)
HALO = 8
FF_COLS = 256
N_FF_TILES = D_FF // FF_COLS
LANES = 128
N_SLAB = FF_COLS // LANES
SCR_PITCH = 2
MM_F32W = dict(tm=2048, tn=512)
MM_DOWN = dict(tm=1024, tn=256)


def _params(*sem):
    return pltpu.CompilerParams(dimension_semantics=sem, vmem_limit_bytes=VMEM_LIMIT)


def _dot(a, b):
    return jnp.dot(a, b, preferred_element_type=F32)


def _as_bf16(x):
    return x if x.dtype == BF16 else x.astype(BF16)


def _rmsnorm_kernel(x_ref, g_ref, o_ref):
    x = x_ref[...]
    r = lax.rsqrt(jnp.mean(x * x, axis=-1, keepdims=True) + RMS_EPS)
    o_ref[...] = (x * r * g_ref[...]).astype(o_ref.dtype)


def _rmsnorm(x, g, out_dtype, tm=512):
    m, d = x.shape
    return pl.pallas_call(
        _rmsnorm_kernel,
        grid=(m // tm,),
        in_specs=[pl.BlockSpec((tm, d), lambda i: (i, 0)),
                  pl.BlockSpec((1, d), lambda i: (0, 0))],
        out_specs=pl.BlockSpec((tm, d), lambda i: (i, 0)),
        out_shape=jax.ShapeDtypeStruct((m, d), out_dtype),
        compiler_params=_params("parallel"),
        name="rmsnorm",
    )(x, g.reshape(1, d))


def _gelu(x):
    return 0.5 * x * (1.0 + lax.erf(x * np.float32(np.sqrt(0.5))))


def _mm_kernel(*refs, n_lhs, epilogue, tm):
    ws = [_as_bf16(refs[n_lhs + l][...]) for l in range(n_lhs)]
    rest = refs[2 * n_lhs:]
    o_ref = rest[-1]
    for r in range(tm // SUB_ROWS):
        rows = slice(r * SUB_ROWS, (r + 1) * SUB_ROWS)
        acc = _dot(refs[0][rows, :], ws[0])
        for l in range(1, n_lhs):
            acc = acc + _dot(refs[l][rows, :], ws[l])
        if epilogue == "residual":
            o_ref[rows, :] = rest[0][rows, :] + acc
        elif epilogue == "gelu":
            o_ref[rows, :] = _gelu(acc)
        else:
            o_ref[rows, :] = acc


def _matmul(lhs_list, w, layer, *, tm, tn, epilogue="none", residual=None, name):
    m = lhs_list[0].shape[0]
    n = w.shape[2]
    in_specs, args = [], []
    for l in lhs_list:
        in_specs.append(pl.BlockSpec((tm, l.shape[1]), lambda i, j: (i, 0),
                                     pipeline_mode=pl.Buffered(1)))
        args.append(l)
    row_off = 0
    for l in lhs_list:
        k = l.shape[1]
        in_specs.append(pl.BlockSpec((None, k, tn),
                                     functools.partial(lambda i, j, rb: (layer, rb, j), rb=row_off // k)))
        args.append(w)
        row_off += k
    if residual is not None:
        in_specs.append(pl.BlockSpec((tm, tn), lambda i, j: (i, j)))
        args.append(residual)
    return pl.pallas_call(
        functools.partial(_mm_kernel, n_lhs=len(lhs_list), epilogue=epilogue, tm=tm),
        grid=(m // tm, n // tn),
        in_specs=in_specs,
        out_specs=pl.BlockSpec((tm, tn), lambda i, j: (i, j)),
        out_shape=jax.ShapeDtypeStruct((m, n), F32),
        compiler_params=_params("parallel", "arbitrary"),
        name=name,
    )(*args)


def _causal_conv3(x, prev, w, seq_len):
    rows, c = x.shape
    nseq = rows // seq_len
    t = lax.broadcasted_iota(jnp.int32, (rows, 1), 0) & (seq_len - 1)
    p0 = jnp.broadcast_to(prev[:, 0:1, :], (nseq, seq_len, c)).reshape(rows, c)
    p1 = jnp.broadcast_to(prev[:, 1:2, :], (nseq, seq_len, c)).reshape(rows, c)
    x1 = jnp.where(t == 0, p1, pltpu.roll(x, 1, 0))
    x2 = jnp.where(t == 0, p0, jnp.where(t == 1, p1, pltpu.roll(x, 2, 0)))
    return x2 * w[0:1, :] + x1 * w[1:2, :] + x * w[2:3, :]


def _last_two_rows(x, seq_len):
    rows, c = x.shape
    return x.reshape(rows // seq_len, seq_len, c)[:, seq_len - 2:, :]


def _conv_a_kernel(gb_ref, gc_ref, hv_ref, w_ref, prev_ref, a_ref, st_ref, *, seq_len):
    xin = gc_ref[...] * hv_ref[...]
    conv = _causal_conv3(xin, prev_ref[...], w_ref[...], seq_len)
    a_ref[...] = (gb_ref[...] * conv).astype(a_ref.dtype)
    st_ref[...] = _last_two_rows(xin, seq_len)


def _conv_a(proj, conv_w, prev, seq_len, tc=256):
    m = proj.shape[0]
    nseq_tile = ROWS // seq_len
    nj = CONV_DIM // tc
    return pl.pallas_call(
        functools.partial(_conv_a_kernel, seq_len=seq_len),
        grid=(m // ROWS, nj),
        in_specs=[pl.BlockSpec((ROWS, tc), lambda i, j: (i, j)),
                  pl.BlockSpec((ROWS, tc), lambda i, j: (i, j + nj)),
                  pl.BlockSpec((ROWS, tc), lambda i, j: (i, j + 2 * nj)),
                  pl.BlockSpec((3, tc), lambda i, j: (0, j)),
                  pl.BlockSpec((nseq_tile, 2, tc), lambda i, j: (i, 0, j))],
        out_specs=[pl.BlockSpec((ROWS, tc), lambda i, j: (i, j)),
                   pl.BlockSpec((nseq_tile, 2, tc), lambda i, j: (i, 0, j))],
        out_shape=[jax.ShapeDtypeStruct((m, CONV_DIM), BF16),
                   jax.ShapeDtypeStruct((m // seq_len, 2, CONV_DIM), F32)],
        compiler_params=_params("parallel", "parallel"),
        name="conv_a",
    )(proj, proj, proj, conv_w, prev)


def _alibi_slopes():
    exps = np.arange(1, N_HEADS + 1, dtype=np.float32) * (8.0 / N_HEADS)
    return np.power(2.0, -exps).astype(np.float32)


def _attn_kernel(sink_ref, q_ref, kh_ref, kn_ref, vh_ref, vn_ref, o_ref, o_acc, *, tq, band_mask):
    nk = kh_ref.shape[0] + kn_ref.shape[0]
    group = N_HEADS // N_KV
    rows = group * tq
    t = lax.broadcasted_iota(jnp.int32, (rows, nk), 0) & (tq - 1)
    s = lax.broadcasted_iota(jnp.int32, (rows, nk), 1)
    neg_dist = -jnp.abs(t + (WINDOW - s)).astype(F32)
    if band_mask:
        q_chunk, k_chunk = t // CHUNK, s // CHUNK
        first_valid = jnp.where(pl.program_id(1) == 0, WINDOW // CHUNK, 0)
        visible = (k_chunk >= q_chunk) & (k_chunk <= q_chunk + WINDOW // CHUNK) & (k_chunk >= first_valid)
    head_in_group = lax.broadcasted_iota(jnp.int32, (rows, 1), 0) // tq
    slopes = _alibi_slopes()

    def per_head_column(vals):
        col = jnp.full((rows, 1), vals[group - 1], F32)
        for g in range(group - 2, -1, -1):
            col = jnp.where(head_in_group == g, vals[g], col)
        return col

    for kh in range(N_KV):
        cols = slice(kh * HEAD_DIM, (kh + 1) * HEAD_DIM)
        heads = range(kh * group, (kh + 1) * group)
        k_all = jnp.concatenate([kh_ref[:, cols], kn_ref[:, cols]], axis=0).astype(BF16)
        v_all = jnp.concatenate([vh_ref[:, cols], vn_ref[:, cols]], axis=0).astype(BF16)
        q = jnp.concatenate([q_ref[:, h * HEAD_DIM:(h + 1) * HEAD_DIM] for h in heads], axis=0).astype(BF16)
        slope = per_head_column([float(slopes[h]) for h in heads])
        sink = per_head_column([sink_ref[h] for h in heads])
        sc = lax.dot_general(q, k_all, (((1,), (1,)), ((), ())), preferred_element_type=F32)
        sc = sc * (HEAD_DIM ** -0.5) + slope * neg_dist
        if band_mask:
            sc = jnp.where(visible, sc, NEG_INF)
        m = jnp.maximum(jnp.max(sc, axis=-1, keepdims=True), sink)
        p = jnp.exp(sc - m)
        denom = jnp.sum(p, axis=-1, keepdims=True) + jnp.exp(sink - m)
        out = _dot((p * (1.0 / denom)).astype(BF16), v_all)
        for g, h in enumerate(heads):
            o_acc[:, h * HEAD_DIM:(h + 1) * HEAD_DIM] = out[g * tq:(g + 1) * tq, :]
    o_ref[...] = o_acc[...].astype(o_ref.dtype)


def _attention(q_src, q_map, k_hist, v_hist, hist_maps, k_new, v_new, new_maps, sinks, *, tq, band_mask, grid):
    m = q_src.shape[0]
    nblk = grid[1]
    return pl.pallas_call(
        functools.partial(_attn_kernel, tq=tq, band_mask=band_mask),
        grid=grid,
        in_specs=[pl.BlockSpec(memory_space=pltpu.SMEM),
                  pl.BlockSpec((tq, Q_DIM), q_map),
                  pl.BlockSpec((WINDOW, KV_DIM), hist_maps[0]), pl.BlockSpec((tq, KV_DIM), new_maps[0]),
                  pl.BlockSpec((WINDOW, KV_DIM), hist_maps[1]), pl.BlockSpec((tq, KV_DIM), new_maps[1])],
        out_specs=pl.BlockSpec((tq, Q_DIM), lambda b, c: (b * nblk + c, 0)),
        out_shape=jax.ShapeDtypeStruct((m, Q_DIM), BF16),
        scratch_shapes=[pltpu.VMEM((tq, Q_DIM), F32)],
        compiler_params=_params("parallel", "parallel"),
        name="swa",
    )(sinks, q_src, k_hist, k_new, v_hist, v_new)


def _gmlp_kernel(u_ref, v_ref, lng_ref, lnb_ref, ws_ref, bs_ref, o_ref, vn_ref, *, cl):
    v = v_ref[...]
    mu = jnp.mean(v, axis=-1, keepdims=True)
    var = jnp.mean(jnp.square(v - mu), axis=-1, keepdims=True)
    vn = (v - mu) * lax.rsqrt(var + LN_EPS) * lng_ref[...] + lnb_ref[...]
    vn_ref[...] = vn
    row = lax.broadcasted_iota(jnp.int32, (cl, cl), 0)
    col = lax.broadcasted_iota(jnp.int32, (cl, cl), 1)
    lower = row >= col
    for g in range(GMLP_GROUPS):
        cols = slice(g * GMLP_GDIM, (g + 1) * GMLP_GDIM)
        ws = jnp.where(lower, ws_ref[g, 0:cl, 0:cl], 0.0).astype(BF16)
        s = _dot(ws, vn[:, cols].astype(BF16)) + bs_ref[0:cl, g:g + 1]
        o_ref[:, cols] = (u_ref[:, cols] * s).astype(o_ref.dtype)


def _gmlp_gate(z, ln_g, ln_b, w_s, b_s_t, cl):
    m = z.shape[0]
    return pl.pallas_call(
        functools.partial(_gmlp_kernel, cl=cl),
        grid=(m // cl,),
        in_specs=[pl.BlockSpec((cl, D_MODEL), lambda i: (i, 0)),
                  pl.BlockSpec((cl, D_MODEL), lambda i: (i, 1)),
                  pl.BlockSpec((1, D_MODEL), lambda i: (0, 0)),
                  pl.BlockSpec((1, D_MODEL), lambda i: (0, 0)),
                  pl.BlockSpec((GMLP_GROUPS, GMLP_CHUNK, GMLP_CHUNK), lambda i: (0, 0, 0)),
                  pl.BlockSpec((GMLP_CHUNK, GMLP_GROUPS), lambda i: (0, 0))],
        out_specs=[pl.BlockSpec((cl, D_MODEL), lambda i: (i, 0)),
                   pl.BlockSpec((cl, D_MODEL), lambda i: (i, 0))],
        out_shape=[jax.ShapeDtypeStruct((m, D_MODEL), BF16),
                   jax.ShapeDtypeStruct((m, D_MODEL), F32)],
        compiler_params=_params("parallel"),
        name="gmlp_gate",
    )(z, z, ln_g.reshape(1, -1), ln_b.reshape(1, -1), w_s, b_s_t)


def _ffn_up_kernel(h_ref, wg_ref, wu_ref, cwg_ref, cwu_ref, bg_ref, bu_ref, pg_ref, pu_ref,
                   act_ref, sg_ref, su_ref, ug_scr, uu_scr, *, seq_len):
    wg, wu = _as_bf16(wg_ref[...]), _as_bf16(wu_ref[...])
    ug_scr[:, 0:SCR_PITCH * HALO, :] = jnp.zeros((N_SLAB, SCR_PITCH * HALO, LANES), F32)
    uu_scr[:, 0:SCR_PITCH * HALO, :] = jnp.zeros((N_SLAB, SCR_PITCH * HALO, LANES), F32)
    row = lax.broadcasted_iota(jnp.int32, (EP_ROWS, 1), 0)

    def rows_at(scr, s, t, n):
        return scr[s, pl.ds(SCR_PITCH * (HALO + t), n, stride=SCR_PITCH), :]

    def conv(scr, prev_ref, cw_ref, b_ref, s, start, z):
        lanes = slice(s * LANES, (s + 1) * LANES)
        x0, x1, x2 = (rows_at(scr, s, start - back, EP_ROWS) for back in range(3))
        if start % seq_len == 0:
            prev = prev_ref[start // seq_len]
            x1 = jnp.where(row == 0, prev[1:2, lanes], x1)
            x2 = jnp.where(row == 0, prev[0:1, lanes], jnp.where(row == 1, prev[1:2, lanes], x2))

        def tap(x, w):
            return (x.reshape(EP_ROWS // 8, 8, LANES) * (w + z)[None]).reshape(EP_ROWS, LANES)

        return (tap(x2, cw_ref[0:1, lanes]) + tap(x1, cw_ref[1:2, lanes]) + tap(x0, cw_ref[2:3, lanes])
                + b_ref[:, lanes])

    zs = [jnp.zeros((8, LANES), F32)] * N_SLAB

    row0 = 0
    for sub_rows in FFN_SUB_TILES:
        h = h_ref[row0:row0 + sub_rows, :]
        up_g, up_u = _dot(h, wg), _dot(h, wu)
        for s in range(N_SLAB):
            dst = pl.ds(SCR_PITCH * (HALO + row0), sub_rows, stride=SCR_PITCH)
            ug_scr[s, dst, :] = up_g[:, s * LANES:(s + 1) * LANES]
            uu_scr[s, dst, :] = up_u[:, s * LANES:(s + 1) * LANES]
        chunk_starts = range(row0, row0 + sub_rows, EP_ROWS)
        row0 += sub_rows
        for start in chunk_starts:
            end = start + EP_ROWS
            for s in range(N_SLAB):
                lanes = slice(s * LANES, (s + 1) * LANES)
                cg = conv(ug_scr, pg_ref, cwg_ref, bg_ref, s, start, zs[s])
                cu = conv(uu_scr, pu_ref, cwu_ref, bu_ref, s, start, zs[s])
                act = cg * jax.nn.sigmoid(cg) * cu
                act_ref[start:end, lanes] = act.astype(act_ref.dtype)
                bits = lax.bitcast_convert_type(act[0:8, :], jnp.uint32)
                zs[s] = ((bits >> 16) >> 16).astype(F32)
                if end % seq_len == 0:
                    sg_ref[end // seq_len - 1, :, lanes] = rows_at(ug_scr, s, end - 2, 2)
                    su_ref[end // seq_len - 1, :, lanes] = rows_at(uu_scr, s, end - 2, 2)


def _ffn_up(h, w_up, conv_w, conv_b, layer, prev, prev_layer, seq_len):
    m = h.shape[0]
    nseq_tile = ROWS // seq_len
    nj = N_FF_TILES
    st_spec_g = pl.BlockSpec((None, nseq_tile, 2, FF_COLS), lambda i, j: (prev_layer, i, 0, j))
    st_spec_u = pl.BlockSpec((None, nseq_tile, 2, FF_COLS), lambda i, j: (prev_layer, i, 0, j + nj))
    conv_b = conv_b.reshape(conv_b.shape[0], 1, -1)
    return pl.pallas_call(
        functools.partial(_ffn_up_kernel, seq_len=seq_len),
        grid=(m // ROWS, nj),
        in_specs=[pl.BlockSpec((ROWS, D_MODEL), lambda i, j: (i, 0), pipeline_mode=pl.Buffered(1)),
                  pl.BlockSpec((None, D_MODEL, FF_COLS), lambda i, j: (layer, 0, j)),
                  pl.BlockSpec((None, D_MODEL, FF_COLS), lambda i, j: (layer, 0, j + nj)),
                  pl.BlockSpec((None, 3, FF_COLS), lambda i, j: (layer, 0, j)),
                  pl.BlockSpec((None, 3, FF_COLS), lambda i, j: (layer, 0, j + nj)),
                  pl.BlockSpec((None, 1, FF_COLS), lambda i, j: (layer, 0, j)),
                  pl.BlockSpec((None, 1, FF_COLS), lambda i, j: (layer, 0, j + nj)),
                  st_spec_g, st_spec_u],
        out_specs=[pl.BlockSpec((ROWS, FF_COLS), lambda i, j: (i, j)),
                   pl.BlockSpec((nseq_tile, 2, FF_COLS), lambda i, j: (i, 0, j)),
                   pl.BlockSpec((nseq_tile, 2, FF_COLS), lambda i, j: (i, 0, j))],
        out_shape=[jax.ShapeDtypeStruct((m, D_FF), BF16),
                   jax.ShapeDtypeStruct((m // seq_len, 2, D_FF), F32),
                   jax.ShapeDtypeStruct((m // seq_len, 2, D_FF), F32)],
        scratch_shapes=[pltpu.VMEM((N_SLAB, SCR_PITCH * (HALO + ROWS), LANES), F32)] * 2,
        compiler_params=_params("parallel", "arbitrary"),
        name="ffn_up",
    )(h, w_up, w_up, conv_w, conv_w, conv_b, conv_b, prev, prev)


def _trunk(x3, conv_a_cache, swa_k_cache, swa_v_cache, ffn_cache, wts):
    bsz, seq_len, _ = x3.shape
    m = bsz * seq_len
    sample = ffn_cache is not None
    x = x3.reshape(m, D_MODEL)

    h = _rmsnorm(x, wts["norm_mix"][0], BF16)
    proj = _matmul([h], wts["w_in_ab"], 0, **MM_F32W, name="in_ab")
    prev_a = conv_a_cache[0] if sample else jnp.zeros((bsz, 2, CONV_DIM), F32)
    a, conv_state = _conv_a(proj, wts["conv_w_a"][0], prev_a, seq_len)

    q_col = 3 * CONV_DIM // Q_DIM
    k_col = (3 * CONV_DIM + Q_DIM) // KV_DIM
    v_col = k_col + 1
    if sample:
        k_hist = swa_k_cache[0].reshape(bsz * WINDOW, KV_DIM)
        v_hist = swa_v_cache[0].reshape(bsz * WINDOW, KV_DIM)
        hist = [lambda b, c: (b, 0)] * 2
        new = [lambda b, c: (b, k_col), lambda b, c: (b, v_col)]
        attn = _attention(proj, lambda b, c: (b, q_col), k_hist, v_hist, hist, proj, proj, new,
                          wts["sinks"][0], tq=seq_len, band_mask=False, grid=(bsz, 1))
    else:
        nblk = seq_len // WINDOW
        def blk(back, col):
            return lambda b, c: (b * nblk + jnp.maximum(c - back, 0), col)
        attn = _attention(proj, blk(0, q_col), proj, proj, [blk(1, k_col), blk(1, v_col)], proj, proj,
                          [blk(0, k_col), blk(0, v_col)], wts["sinks"][0], tq=WINDOW, band_mask=True,
                          grid=(bsz, nblk))

    k_new = proj[:, k_col * KV_DIM:(k_col + 1) * KV_DIM].reshape(bsz, seq_len, N_KV, HEAD_DIM)
    v_new = proj[:, v_col * KV_DIM:(v_col + 1) * KV_DIM].reshape(bsz, seq_len, N_KV, HEAD_DIM)
    if sample:
        k_state = jnp.concatenate([swa_k_cache[0], k_new], axis=1)[:, seq_len:]
        v_state = jnp.concatenate([swa_v_cache[0], v_new], axis=1)[:, seq_len:]
    else:
        k_state, v_state = k_new[:, seq_len - WINDOW:], v_new[:, seq_len - WINDOW:]

    x = _matmul([a, attn], wts["w_out_ab"], 0, **MM_F32W, epilogue="residual", residual=x,
                name="out_ab")

    ffn_states = []

    def conv_ffn(x, layer):
        h = _rmsnorm(x, wts["norm_ffn"][layer], BF16)
        prev, prev_layer = (ffn_cache, layer) if sample else (jnp.zeros((1, bsz, 2, 2 * D_FF), F32), 0)
        act, st_g, st_u = _ffn_up(h, wts["w_up_ffn"], wts["conv_w_ffn"], wts["conv_b_ffn"], layer,
                                  prev, prev_layer, seq_len)
        ffn_states.append(jnp.concatenate([st_g, st_u], axis=-1))
        return _matmul([act], wts["w_down_ffn"], layer, **MM_DOWN, epilogue="residual",
                       residual=x, name="ffn_down")

    x = conv_ffn(x, 0)

    h = _rmsnorm(x, wts["norm_mix"][1], BF16)
    z = _matmul([h], wts["w_in_c"], 0, **MM_F32W, epilogue="gelu", name="in_c")
    cl = min(seq_len, GMLP_CHUNK)
    gated, vn = _gmlp_gate(z, wts["ln_g_c"][0], wts["ln_b_c"][0], wts["w_s_c"][0], wts["b_s_c"][0].T, cl)
    x = _matmul([gated], wts["w_out_c"], 0, **MM_F32W, epilogue="residual", residual=x,
                name="out_c")
    x = conv_ffn(x, 1)

    y = _rmsnorm(x, wts["norm_final"], F32)
    return (y.reshape(bsz, seq_len, D_MODEL), conv_state[None], k_state[None], v_state[None],
            jnp.stack(ffn_states), vn.reshape(1, bsz, seq_len, D_MODEL))


def kernel(x_prompt, x_sample, state_conv_a, cache_swa_k, cache_swa_v, state_ffn_conv, norm_mix, norm_ffn,
           norm_final, w_in_ab, conv_w_a, sinks, w_out_ab, w_in_c, ln_g_c, ln_b_c, w_s_c, b_s_c, w_out_c,
           w_up_ffn, conv_w_ffn, conv_b_ffn, w_down_ffn):
    wts = dict(norm_mix=norm_mix, norm_ffn=norm_ffn, norm_final=norm_final,
               w_in_ab=w_in_ab, conv_w_a=conv_w_a, sinks=sinks, w_out_ab=w_out_ab,
               w_in_c=w_in_c, ln_g_c=ln_g_c, ln_b_c=ln_b_c, w_s_c=w_s_c, b_s_c=b_s_c,
               w_out_c=w_out_c, w_up_ffn=w_up_ffn, conv_w_ffn=conv_w_ffn,
               conv_b_ffn=conv_b_ffn, w_down_ffn=w_down_ffn.astype(BF16))
    y_p, conv_p, k_p, v_p, ffn_p, _ = _trunk(x_prompt, None, None, None, None, wts)
    y_s, conv_s, k_s, v_s, ffn_s, gv_s = _trunk(x_sample, state_conv_a, cache_swa_k, cache_swa_v,
                                                state_ffn_conv, wts)
    return (y_p, y_s, conv_p, conv_s, k_p, k_s, v_p, v_s, ffn_p, ffn_s, gv_s)
```

```python
import functools

import numpy as np
import jax
import jax.numpy as jnp
from jax import lax
from jax.experimental import pallas as pl
from jax.experimental.pallas import tpu as pltpu

F32 = jnp.float32
BF16 = jnp.bfloat16

D_MODEL = 4096
CHUNK = 64
CONV_DIM = 2048
N_HEADS = 32
N_KV = 8
HEAD_DIM = 64
Q_DIM = N_HEADS * HEAD_DIM
KV_DIM = N_KV * HEAD_DIM
WINDOW = 128
IN_AB_DIM = 3 * CONV_DIM + Q_DIM + 2 * KV_DIM
GMLP_CHUNK = 128
GMLP_GROUPS = 16
GMLP_GDIM = D_MODEL // GMLP_GROUPS
D_FF = 11008
RMS_EPS = 1e-6
LN_EPS = 1e-5
NEG_INF = -1e30

V7X_VMEM_BYTES = 64 * 1024 * 1024
VMEM_LIMIT = V7X_VMEM_BYTES - 8 * 1024 * 1024

ROWS = 2048
SUB_ROWS = 512
ATTN_ROWS = 128
EP_ROWS = 32
FFN_SUB_TILES = (512, 512, 512, 320, 192)
HALO = 8
FF_COLS = 256
N_FF_TILES = D_FF // FF_COLS
LANES = 128
N_SLAB = FF_COLS // LANES
SCR_PITCH = 2
MM_F32W = dict(tm=2048, tn=512, lhs_buffers=1)
MM_DOWN = dict(tm=512, tn=512, lhs_buffers=2)


def _params(*sem):
    return pltpu.CompilerParams(dimension_semantics=sem, vmem_limit_bytes=VMEM_LIMIT)


def _dot(a, b):
    return jnp.dot(a, b, preferred_element_type=F32)


def _as_bf16(x):
    return x if x.dtype == BF16 else x.astype(BF16)


def _rmsnorm_kernel(x_ref, g_ref, o_ref):
    x = x_ref[...]
    r = lax.rsqrt(jnp.mean(x * x, axis=-1, keepdims=True) + RMS_EPS)
    o_ref[...] = (x * r * g_ref[...]).astype(o_ref.dtype)


def _rmsnorm(x, g, out_dtype, tm=512):
    m, d = x.shape
    return pl.pallas_call(
        _rmsnorm_kernel,
        grid=(m // tm,),
        in_specs=[pl.BlockSpec((tm, d), lambda i: (i, 0)),
                  pl.BlockSpec((1, d), lambda i: (0, 0))],
        out_specs=pl.BlockSpec((tm, d), lambda i: (i, 0)),
        out_shape=jax.ShapeDtypeStruct((m, d), out_dtype),
        compiler_params=_params("parallel"),
        name="rmsnorm",
    )(x, g.reshape(1, d))


def _gelu(x):
    return 0.5 * x * (1.0 + lax.erf(x * np.float32(np.sqrt(0.5))))


def _mm_kernel(*refs, n_lhs, epilogue, tm):
    ws = [_as_bf16(refs[n_lhs + l][...]) for l in range(n_lhs)]
    rest = refs[2 * n_lhs:]
    o_ref = rest[-1]
    for r in range(tm // SUB_ROWS):
        rows = slice(r * SUB_ROWS, (r + 1) * SUB_ROWS)
        acc = _dot(refs[0][rows, :], ws[0])
        for l in range(1, n_lhs):
            acc = acc + _dot(refs[l][rows, :], ws[l])
        if epilogue == "residual":
            o_ref[rows, :] = rest[0][rows, :] + acc
        elif epilogue == "gelu":
            o_ref[rows, :] = _gelu(acc)
        else:
            o_ref[rows, :] = acc


def _matmul(lhs_list, w, layer, *, tm, tn, lhs_buffers, epilogue="none", residual=None, name):
    m = lhs_list[0].shape[0]
    n = w.shape[2]
    in_specs, args = [], []
    for l in lhs_list:
        in_specs.append(pl.BlockSpec((tm, l.shape[1]), lambda i, j: (i, 0),
                                     pipeline_mode=pl.Buffered(lhs_buffers)))
        args.append(l)
    row_off = 0
    for l in lhs_list:
        k = l.shape[1]
        in_specs.append(pl.BlockSpec((None, k, tn),
                                     functools.partial(lambda i, j, rb: (layer, rb, j), rb=row_off // k)))
        args.append(w)
        row_off += k
    if residual is not None:
        in_specs.append(pl.BlockSpec((tm, tn), lambda i, j: (i, j)))
        args.append(residual)
    return pl.pallas_call(
        functools.partial(_mm_kernel, n_lhs=len(lhs_list), epilogue=epilogue, tm=tm),
        grid=(m // tm, n // tn),
        in_specs=in_specs,
        out_specs=pl.BlockSpec((tm, tn), lambda i, j: (i, j)),
        out_shape=jax.ShapeDtypeStruct((m, n), F32),
        compiler_params=_params("parallel", "arbitrary"),
        name=name,
    )(*args)


def _causal_conv3(x, prev, w, seq_len):
    rows, c = x.shape
    nseq = rows // seq_len
    t = lax.broadcasted_iota(jnp.int32, (rows, 1), 0) & (seq_len - 1)
    p0 = jnp.broadcast_to(prev[:, 0:1, :], (nseq, seq_len, c)).reshape(rows, c)
    p1 = jnp.broadcast_to(prev[:, 1:2, :], (nseq, seq_len, c)).reshape(rows, c)
    x1 = jnp.where(t == 0, p1, pltpu.roll(x, 1, 0))
    x2 = jnp.where(t == 0, p0, jnp.where(t == 1, p1, pltpu.roll(x, 2, 0)))
    return x2 * w[0:1, :] + x1 * w[1:2, :] + x * w[2:3, :]


def _last_two_rows(x, seq_len):
    rows, c = x.shape
    return x.reshape(rows // seq_len, seq_len, c)[:, seq_len - 2:, :]


def _conv_a_kernel(gb_ref, gc_ref, hv_ref, w_ref, prev_ref, a_ref, st_ref, *, seq_len):
    xin = gc_ref[...] * hv_ref[...]
    conv = _causal_conv3(xin, prev_ref[...], w_ref[...], seq_len)
    a_ref[...] = (gb_ref[...] * conv).astype(a_ref.dtype)
    st_ref[...] = _last_two_rows(xin, seq_len)


def _conv_a(proj, conv_w, prev, seq_len, tc=256):
    m = proj.shape[0]
    nseq_tile = ROWS // seq_len
    nj = CONV_DIM // tc
    return pl.pallas_call(
        functools.partial(_conv_a_kernel, seq_len=seq_len),
        grid=(m // ROWS, nj),
        in_specs=[pl.BlockSpec((ROWS, tc), lambda i, j: (i, j)),
                  pl.BlockSpec((ROWS, tc), lambda i, j: (i, j + nj)),
                  pl.BlockSpec((ROWS, tc), lambda i, j: (i, j + 2 * nj)),
                  pl.BlockSpec((3, tc), lambda i, j: (0, j)),
                  pl.BlockSpec((nseq_tile, 2, tc), lambda i, j: (i, 0, j))],
        out_specs=[pl.BlockSpec((ROWS, tc), lambda i, j: (i, j)),
                   pl.BlockSpec((nseq_tile, 2, tc), lambda i, j: (i, 0, j))],
        out_shape=[jax.ShapeDtypeStruct((m, CONV_DIM), BF16),
                   jax.ShapeDtypeStruct((m // seq_len, 2, CONV_DIM), F32)],
        compiler_params=_params("parallel", "parallel"),
        name="conv_a",
    )(proj, proj, proj, conv_w, prev)


def _alibi_slopes():
    exps = np.arange(1, N_HEADS + 1, dtype=np.float32) * (8.0 / N_HEADS)
    return np.power(2.0, -exps).astype(np.float32)


def _attn_kernel(sink_ref, q_ref, kh_ref, kn_ref, vh_ref, vn_ref, o_ref, o_acc, bias_scr, sc_scr, w_scr,
                 *, tq, band_mask):
    nk = kh_ref.shape[0] + kn_ref.shape[0]
    group = N_HEADS // N_KV
    rows = group * tq
    slopes = _alibi_slopes()
    n_var = bias_scr.shape[0]

    @pl.when((pl.program_id(0) == 0) & (pl.program_id(1) == 0))
    def _():
        t = lax.broadcasted_iota(jnp.int32, (tq, nk), 0)
        s = lax.broadcasted_iota(jnp.int32, (tq, nk), 1)
        neg_dist = -jnp.abs(t + (WINDOW - s)).astype(F32)
        q_chunk, k_chunk = t // CHUNK, s // CHUNK
        for var in range(n_var):
            for h in range(N_HEADS):
                bias = float(slopes[h]) * neg_dist
                if band_mask:
                    first_valid = WINDOW // CHUNK if var == 1 else 0
                    visible = ((k_chunk >= q_chunk) & (k_chunk <= q_chunk + WINDOW // CHUNK)
                               & (k_chunk >= first_valid))
                    bias = jnp.where(visible, bias, NEG_INF)
                bias_scr[var, h] = bias

    var = jnp.where(pl.program_id(1) == 0, n_var - 1, 0)
    rc = min(tq, ATTN_ROWS)

    for kh in range(N_KV):
        cols = slice(kh * HEAD_DIM, (kh + 1) * HEAD_DIM)
        k_all = jnp.concatenate([kh_ref[:, cols], kn_ref[:, cols]], axis=0).astype(BF16)
        q = jnp.concatenate([q_ref[:, h * HEAD_DIM:(h + 1) * HEAD_DIM]
                             for h in range(kh * group, (kh + 1) * group)], axis=0)
        q = (q * (HEAD_DIM ** -0.5)).astype(BF16)
        sc_scr[kh] = lax.dot_general(q, k_all, (((1,), (1,)), ((), ())), preferred_element_type=F32)
    for h in range(N_HEADS):
        kh, g = divmod(h, group)
        sink = sink_ref[h]
        for r0 in range(0, tq, rc):
            rws = slice(g * tq + r0, g * tq + r0 + rc)
            s = sc_scr[kh, rws, :] + bias_scr[var, h, r0:r0 + rc, :]
            m = jnp.maximum(jnp.max(s, axis=-1, keepdims=True), sink)
            p = jnp.exp(s - m)
            denom = jnp.sum(p, axis=-1, keepdims=True) + jnp.exp(sink - m)
            w_scr[kh, rws, :] = (p * (1.0 / denom)).astype(BF16)
    for kh in range(N_KV):
        cols = slice(kh * HEAD_DIM, (kh + 1) * HEAD_DIM)
        v_all = jnp.concatenate([vh_ref[:, cols], vn_ref[:, cols]], axis=0).astype(BF16)
        out = _dot(w_scr[kh], v_all)
        for g in range(group):
            h = kh * group + g
            o_acc[:, h * HEAD_DIM:(h + 1) * HEAD_DIM] = out[g * tq:(g + 1) * tq, :]
    o_ref[...] = o_acc[...].astype(o_ref.dtype)


def _attention(q_src, q_map, k_hist, v_hist, hist_maps, k_new, v_new, new_maps, sinks, *, tq, band_mask, grid):
    m = q_src.shape[0]
    nblk = grid[1]
    return pl.pallas_call(
        functools.partial(_attn_kernel, tq=tq, band_mask=band_mask),
        grid=grid,
        in_specs=[pl.BlockSpec(memory_space=pltpu.SMEM),
                  pl.BlockSpec((tq, Q_DIM), q_map),
                  pl.BlockSpec((WINDOW, KV_DIM), hist_maps[0]), pl.BlockSpec((tq, KV_DIM), new_maps[0]),
                  pl.BlockSpec((WINDOW, KV_DIM), hist_maps[1]), pl.BlockSpec((tq, KV_DIM), new_maps[1])],
        out_specs=pl.BlockSpec((tq, Q_DIM), lambda b, c: (b * nblk + c, 0)),
        out_shape=jax.ShapeDtypeStruct((m, Q_DIM), BF16),
        scratch_shapes=[pltpu.VMEM((tq, Q_DIM), F32),
                        pltpu.VMEM((2 if band_mask else 1, N_HEADS, tq, WINDOW + tq), F32),
                        pltpu.VMEM((N_KV, N_HEADS // N_KV * tq, WINDOW + tq), F32),
                        pltpu.VMEM((N_KV, N_HEADS // N_KV * tq, WINDOW + tq), BF16)],
        compiler_params=_params("arbitrary", "arbitrary"),
        name="swa",
    )(sinks, q_src, k_hist, k_new, v_hist, v_new)


def _gmlp_kernel(u_ref, v_ref, lng_ref, lnb_ref, ws_ref, bs_ref, o_ref, *maybe_vn_ref, cl):
    v = v_ref[...]
    mu = jnp.mean(v, axis=-1, keepdims=True)
    var = jnp.mean(jnp.square(v - mu), axis=-1, keepdims=True)
    vn = (v - mu) * lax.rsqrt(var + LN_EPS) * lng_ref[...] + lnb_ref[...]
    for vn_ref in maybe_vn_ref:
        vn_ref[...] = vn
    row = lax.broadcasted_iota(jnp.int32, (cl, cl), 0)
    col = lax.broadcasted_iota(jnp.int32, (cl, cl), 1)
    lower = row >= col
    for g in range(GMLP_GROUPS):
        cols = slice(g * GMLP_GDIM, (g + 1) * GMLP_GDIM)
        ws = jnp.where(lower, ws_ref[g, 0:cl, 0:cl], 0.0).astype(BF16)
        s = _dot(ws, vn[:, cols].astype(BF16)) + bs_ref[0:cl, g:g + 1]
        o_ref[:, cols] = (u_ref[:, cols] * s).astype(o_ref.dtype)


def _gmlp_gate(z, ln_g, ln_b, w_s, b_s_t, cl, want_vn):
    m = z.shape[0]
    n_out = 2 if want_vn else 1
    return pl.pallas_call(
        functools.partial(_gmlp_kernel, cl=cl),
        grid=(m // cl,),
        in_specs=[pl.BlockSpec((cl, D_MODEL), lambda i: (i, 0)),
                  pl.BlockSpec((cl, D_MODEL), lambda i: (i, 1)),
                  pl.BlockSpec((1, D_MODEL), lambda i: (0, 0)),
                  pl.BlockSpec((1, D_MODEL), lambda i: (0, 0)),
                  pl.BlockSpec((GMLP_GROUPS, GMLP_CHUNK, GMLP_CHUNK), lambda i: (0, 0, 0)),
                  pl.BlockSpec((GMLP_CHUNK, GMLP_GROUPS), lambda i: (0, 0))],
        out_specs=[pl.BlockSpec((cl, D_MODEL), lambda i: (i, 0)),
                   pl.BlockSpec((cl, D_MODEL), lambda i: (i, 0))][:n_out],
        out_shape=[jax.ShapeDtypeStruct((m, D_MODEL), BF16),
                   jax.ShapeDtypeStruct((m, D_MODEL), F32)][:n_out],
        compiler_params=_params("parallel"),
        name="gmlp_gate",
    )(z, z, ln_g.reshape(1, -1), ln_b.reshape(1, -1), w_s, b_s_t)


def _ffn_up_kernel(h_ref, wg_ref, wu_ref, cwg_ref, cwu_ref, bg_ref, bu_ref, pg_ref, pu_ref,
                   act_ref, sg_ref, su_ref, ug_scr, uu_scr, *, seq_len):
    wg, wu = _as_bf16(wg_ref[...]), _as_bf16(wu_ref[...])
    ug_scr[:, 0:SCR_PITCH * HALO, :] = jnp.zeros((N_SLAB, SCR_PITCH * HALO, LANES), F32)
    uu_scr[:, 0:SCR_PITCH * HALO, :] = jnp.zeros((N_SLAB, SCR_PITCH * HALO, LANES), F32)
    row = lax.broadcasted_iota(jnp.int32, (EP_ROWS, 1), 0)

    def rows_at(scr, s, t, n):
        return scr[s, pl.ds(SCR_PITCH * (HALO + t), n, stride=SCR_PITCH), :]

    def conv(scr, prev_ref, cw_ref, b_ref, s, start, z):
        lanes = slice(s * LANES, (s + 1) * LANES)
        x0, x1, x2 = (rows_at(scr, s, start - back, EP_ROWS) for back in range(3))
        if start % seq_len == 0:
            prev = prev_ref[start // seq_len]
            x1 = jnp.where(row == 0, prev[1:2, lanes], x1)
            x2 = jnp.where(row == 0, prev[0:1, lanes], jnp.where(row == 1, prev[1:2, lanes], x2))

        def tap(x, w):
            return (x.reshape(EP_ROWS // 8, 8, LANES) * (w + z)[None]).reshape(EP_ROWS, LANES)

        return (tap(x2, cw_ref[0:1, lanes]) + tap(x1, cw_ref[1:2, lanes]) + tap(x0, cw_ref[2:3, lanes])
                + b_ref[:, lanes])

    zs = [jnp.zeros((8, LANES), F32)] * N_SLAB

    row0 = 0
    for sub_rows in FFN_SUB_TILES:
        h = h_ref[row0:row0 + sub_rows, :]
        up_g, up_u = _dot(h, wg), _dot(h, wu)
        for s in range(N_SLAB):
            dst = pl.ds(SCR_PITCH * (HALO + row0), sub_rows, stride=SCR_PITCH)
            ug_scr[s, dst, :] = up_g[:, s * LANES:(s + 1) * LANES]
            uu_scr[s, dst, :] = up_u[:, s * LANES:(s + 1) * LANES]
        chunk_starts = range(row0, row0 + sub_rows, EP_ROWS)
        row0 += sub_rows
        for start in chunk_starts:
            end = start + EP_ROWS
            for s in range(N_SLAB):
                lanes = slice(s * LANES, (s + 1) * LANES)
                cg = conv(ug_scr, pg_ref, cwg_ref, bg_ref, s, start, zs[s])
                cu = conv(uu_scr, pu_ref, cwu_ref, bu_ref, s, start, zs[s])
                act = cg * jax.nn.sigmoid(cg) * cu
                act_ref[start:end, lanes] = act.astype(act_ref.dtype)
                bits = lax.bitcast_convert_type(act[0:8, :], jnp.uint32)
                zs[s] = ((bits >> 16) >> 16).astype(F32)
                if end % seq_len == 0:
                    sg_ref[end // seq_len - 1, :, lanes] = rows_at(ug_scr, s, end - 2, 2)
                    su_ref[end // seq_len - 1, :, lanes] = rows_at(uu_scr, s, end - 2, 2)


def _ffn_up(h, w_up, conv_w, conv_b, layer, prev, prev_layer, seq_len):
    m = h.shape[0]
    nseq_tile = ROWS // seq_len
    nj = N_FF_TILES
    st_spec_g = pl.BlockSpec((None, nseq_tile, 2, FF_COLS), lambda i, j: (prev_layer, i, 0, j))
    st_spec_u = pl.BlockSpec((None, nseq_tile, 2, FF_COLS), lambda i, j: (prev_layer, i, 0, j + nj))
    conv_b = conv_b.reshape(conv_b.shape[0], 1, -1)
    return pl.pallas_call(
        functools.partial(_ffn_up_kernel, seq_len=seq_len),
        grid=(m // ROWS, nj),
        in_specs=[pl.BlockSpec((ROWS, D_MODEL), lambda i, j: (i, 0), pipeline_mode=pl.Buffered(1)),
                  pl.BlockSpec((None, D_MODEL, FF_COLS), lambda i, j: (layer, 0, j)),
                  pl.BlockSpec((None, D_MODEL, FF_COLS), lambda i, j: (layer, 0, j + nj)),
                  pl.BlockSpec((None, 3, FF_COLS), lambda i, j: (layer, 0, j)),
                  pl.BlockSpec((None, 3, FF_COLS), lambda i, j: (layer, 0, j + nj)),
                  pl.BlockSpec((None, 1, FF_COLS), lambda i, j: (layer, 0, j)),
                  pl.BlockSpec((None, 1, FF_COLS), lambda i, j: (layer, 0, j + nj)),
                  st_spec_g, st_spec_u],
        out_specs=[pl.BlockSpec((ROWS, FF_COLS), lambda i, j: (i, j)),
                   pl.BlockSpec((nseq_tile, 2, FF_COLS), lambda i, j: (i, 0, j)),
                   pl.BlockSpec((nseq_tile, 2, FF_COLS), lambda i, j: (i, 0, j))],
        out_shape=[jax.ShapeDtypeStruct((m, D_FF), BF16),
                   jax.ShapeDtypeStruct((m // seq_len, 2, D_FF), F32),
                   jax.ShapeDtypeStruct((m // seq_len, 2, D_FF), F32)],
        scratch_shapes=[pltpu.VMEM((N_SLAB, SCR_PITCH * (HALO + ROWS), LANES), F32)] * 2,
        compiler_params=_params("parallel", "arbitrary"),
        name="ffn_up",
    )(h, w_up, w_up, conv_w, conv_w, conv_b, conv_b, prev, prev)


def _trunk(x3, conv_a_cache, swa_k_cache, swa_v_cache, ffn_cache, wts):
    bsz, seq_len, _ = x3.shape
    m = bsz * seq_len
    sample = ffn_cache is not None
    x = x3.reshape(m, D_MODEL)

    h = _rmsnorm(x, wts["norm_mix"][0], BF16)
    proj = _matmul([h], wts["w_in_ab"], 0, **MM_F32W, name="in_ab")
    prev_a = conv_a_cache[0] if sample else jnp.zeros((bsz, 2, CONV_DIM), F32)
    a, conv_state = _conv_a(proj, wts["conv_w_a"][0], prev_a, seq_len)

    q_col = 3 * CONV_DIM // Q_DIM
    k_col = (3 * CONV_DIM + Q_DIM) // KV_DIM
    v_col = k_col + 1
    if sample:
        k_hist = swa_k_cache[0].reshape(bsz * WINDOW, KV_DIM)
        v_hist = swa_v_cache[0].reshape(bsz * WINDOW, KV_DIM)
        hist = [lambda b, c: (b, 0)] * 2
        new = [lambda b, c: (b, k_col), lambda b, c: (b, v_col)]
        attn = _attention(proj, lambda b, c: (b, q_col), k_hist, v_hist, hist, proj, proj, new,
                          wts["sinks"][0], tq=seq_len, band_mask=False, grid=(bsz, 1))
    else:
        nblk = seq_len // WINDOW
        def blk(back, col):
            return lambda b, c: (b * nblk + jnp.maximum(c - back, 0), col)
        attn = _attention(proj, blk(0, q_col), proj, proj, [blk(1, k_col), blk(1, v_col)], proj, proj,
                          [blk(0, k_col), blk(0, v_col)], wts["sinks"][0], tq=WINDOW, band_mask=True,
                          grid=(bsz, nblk))

    k_new = proj[:, k_col * KV_DIM:(k_col + 1) * KV_DIM].reshape(bsz, seq_len, N_KV, HEAD_DIM)
    v_new = proj[:, v_col * KV_DIM:(v_col + 1) * KV_DIM].reshape(bsz, seq_len, N_KV, HEAD_DIM)
    if sample:
        k_state = jnp.concatenate([swa_k_cache[0], k_new], axis=1)[:, seq_len:]
        v_state = jnp.concatenate([swa_v_cache[0], v_new], axis=1)[:, seq_len:]
    else:
        k_state, v_state = k_new[:, seq_len - WINDOW:], v_new[:, seq_len - WINDOW:]

    x = _matmul([a, attn], wts["w_out_ab"], 0, **MM_F32W, epilogue="residual", residual=x,
                name="out_ab")

    ffn_states = []

    def conv_ffn(x, layer):
        h = _rmsnorm(x, wts["norm_ffn"][layer], BF16)
        prev, prev_layer = (ffn_cache, layer) if sample else (jnp.zeros((1, bsz, 2, 2 * D_FF), F32), 0)
        act, st_g, st_u = _ffn_up(h, wts["w_up_ffn"], wts["conv_w_ffn"], wts["conv_b_ffn"], layer,
                                  prev, prev_layer, seq_len)
        ffn_states.append(jnp.concatenate([st_g, st_u], axis=-1))
        return _matmul([act], wts["w_down_ffn"], layer, **MM_DOWN, epilogue="residual",
                       residual=x, name="ffn_down")

    x = conv_ffn(x, 0)

    h = _rmsnorm(x, wts["norm_mix"][1], BF16)
    z = _matmul([h], wts["w_in_c"], 0, **MM_F32W, epilogue="gelu", name="in_c")
    cl = min(seq_len, GMLP_CHUNK)
    gated, *maybe_vn = _gmlp_gate(z, wts["ln_g_c"][0], wts["ln_b_c"][0], wts["w_s_c"][0], wts["b_s_c"][0].T,
                                  cl, want_vn=sample)
    gv_state = maybe_vn[0].reshape(1, bsz, seq_len, D_MODEL) if sample else None
    x = _matmul([gated], wts["w_out_c"], 0, **MM_F32W, epilogue="residual", residual=x,
                name="out_c")
    x = conv_ffn(x, 1)

    y = _rmsnorm(x, wts["norm_final"], F32)
    return (y.reshape(bsz, seq_len, D_MODEL), conv_state[None], k_state[None], v_state[None],
            jnp.stack(ffn_states), gv_state)


def kernel(x_prompt, x_sample, state_conv_a, cache_swa_k, cache_swa_v, state_ffn_conv, norm_mix, norm_ffn,
           norm_final, w_in_ab, conv_w_a, sinks, w_out_ab, w_in_c, ln_g_c, ln_b_c, w_s_c, b_s_c, w_out_c,
           w_up_ffn, conv_w_ffn, conv_b_ffn, w_down_ffn):
    wts = dict(norm_mix=norm_mix, norm_ffn=norm_ffn, norm_final=norm_final,
               w_in_ab=w_in_ab, conv_w_a=conv_w_a, sinks=sinks, w_out_ab=w_out_ab,
               w_in_c=w_in_c, ln_g_c=ln_g_c, ln_b_c=ln_b_c, w_s_c=w_s_c, b_s_c=b_s_c,
               w_out_c=w_out_c, w_up_ffn=w_up_ffn, conv_w_ffn=conv_w_ffn,
               conv_b_ffn=conv_b_ffn, w_down_ffn=w_down_ffn.astype(BF16))
    y_p, conv_p, k_p, v_p, ffn_p, _ = _trunk(x_prompt, None, None, None, None, wts)
    y_s, conv_s, k_s, v_s, ffn_s, gv_s = _trunk(x_sample, state_conv_a, cache_swa_k, cache_swa_v,
                                                state_ffn_conv, wts)
    return (y_p, y_s, conv_p, conv_s, k_p, k_s, v_p, v_s, ffn_p, ffn_s, gv_s)
```

```python
import functools

import numpy as np
import jax
import jax.numpy as jnp
from jax import lax
from jax.experimental import pallas as pl
from jax.experimental.pallas import tpu as pltpu

F32 = jnp.float32
BF16 = jnp.bfloat16

D_MODEL = 4096
CHUNK = 64
CONV_DIM = 2048
N_HEADS = 32
N_KV = 8
HEAD_DIM = 64
Q_DIM = N_HEADS * HEAD_DIM
KV_DIM = N_KV * HEAD_DIM
WINDOW = 128
IN_AB_DIM = 3 * CONV_DIM + Q_DIM + 2 * KV_DIM
GMLP_CHUNK = 128
GMLP_GROUPS = 16
GMLP_GDIM = D_MODEL // GMLP_GROUPS
D_FF = 11008
RMS_EPS = 1e-6
LN_EPS = 1e-5
NEG_INF = -1e30

V7X_VMEM_BYTES = 64 * 1024 * 1024
VMEM_LIMIT = V7X_VMEM_BYTES - 8 * 1024 * 1024

ROWS = 2048
SUB_ROWS = 512
ATTN_ROWS = 128
EP_ROWS = 32
FFN_SUB_TILES = (512, 512, 512, 320, 192)
HALO = 8
FF_COLS = 256
N_FF_TILES = D_FF // FF_COLS
LANES = 128
N_SLAB = FF_COLS // LANES
SCR_PITCH = 2
MM_F32W = dict(tm=2048, tn=512, lhs_buffers=1)
MM_DOWN = dict(tm=512, tn=512, lhs_buffers=2)


def _params(*sem):
    return pltpu.CompilerParams(dimension_semantics=sem, vmem_limit_bytes=VMEM_LIMIT)


def _dot(a, b):
    return jnp.dot(a, b, preferred_element_type=F32)


def _as_bf16(x):
    return x if x.dtype == BF16 else x.astype(BF16)


def _rmsnorm_kernel(x_ref, g_ref, o_ref):
    x = x_ref[...]
    r = lax.rsqrt(jnp.mean(x * x, axis=-1, keepdims=True) + RMS_EPS)
    o_ref[...] = (x * r * g_ref[...]).astype(o_ref.dtype)


def _rmsnorm(x, g, out_dtype, tm=512):
    m, d = x.shape
    return pl.pallas_call(
        _rmsnorm_kernel,
        grid=(m // tm,),
        in_specs=[pl.BlockSpec((tm, d), lambda i: (i, 0)),
                  pl.BlockSpec((1, d), lambda i: (0, 0))],
        out_specs=pl.BlockSpec((tm, d), lambda i: (i, 0)),
        out_shape=jax.ShapeDtypeStruct((m, d), out_dtype),
        compiler_params=_params("parallel"),
        name="rmsnorm",
    )(x, g.reshape(1, d))


def _gelu(x):
    return 0.5 * x * (1.0 + lax.erf(x * np.float32(np.sqrt(0.5))))


def _row_scale(ss):
    ms = jnp.sum(ss, axis=-1, keepdims=True) * (1.0 / D_MODEL)
    return jnp.broadcast_to(lax.rsqrt(ms + RMS_EPS), ss.shape)


def _per_lane_tile(fn, x):
    return jnp.concatenate([fn(x[:, k * LANES:(k + 1) * LANES]) for k in range(x.shape[1] // LANES)], axis=1)


def _lane_partial_sumsq(x):
    sq = x * x
    part = sq[:, 0:LANES]
    for k in range(1, sq.shape[1] // LANES):
        part = part + sq[:, k * LANES:(k + 1) * LANES]
    return part


def _mm_kernel(*refs, n_lhs, epilogue, tm, scale_rows, emit_norm):
    refs = list(refs)
    lhs, w_refs = refs[:n_lhs], refs[n_lhs:2 * n_lhs]
    rest = refs[2 * n_lhs:]
    ss_in = rest.pop(0) if scale_rows else None
    res_ref = rest.pop(0) if epilogue == "residual" else None
    gain_ref = rest.pop(0) if emit_norm else None
    o_ref = rest.pop(0)
    xg_ref, ss_out = (rest.pop(0), rest.pop(0)) if emit_norm else (None, None)
    r_scr = rest.pop(0) if scale_rows else None
    first_col = pl.program_id(1) == 0

    if scale_rows:
        @pl.when(first_col)
        def _():
            r_scr[...] = _row_scale(ss_in[...])
    if emit_norm:
        @pl.when(first_col)
        def _():
            ss_out[...] = jnp.zeros(ss_out.shape, F32)

    ws = [_as_bf16(w[...]) for w in w_refs]
    for r in range(tm // SUB_ROWS):
        rows = slice(r * SUB_ROWS, (r + 1) * SUB_ROWS)
        acc = _dot(lhs[0][rows, :], ws[0])
        for l in range(1, n_lhs):
            acc = acc + _dot(lhs[l][rows, :], ws[l])
        if scale_rows:
            scale = r_scr[rows, :]
            acc = _per_lane_tile(lambda t: t * scale, acc)
        if epilogue == "residual":
            acc = res_ref[rows, :] + acc
        elif epilogue == "gelu":
            acc = _gelu(acc)
        o_ref[rows, :] = acc
        if emit_norm:
            xg_ref[rows, :] = (acc * gain_ref[...]).astype(BF16)
            ss_out[rows, :] += _lane_partial_sumsq(acc)


def _matmul(lhs_list, w, layer, *, tm, tn, lhs_buffers, epilogue="none", residual=None, row_ss=None,
            next_gain=None, name):
    m = lhs_list[0].shape[0]
    n = w.shape[2]
    in_specs, args = [], []
    for l in lhs_list:
        in_specs.append(pl.BlockSpec((tm, l.shape[1]), lambda i, j: (i, 0),
                                     pipeline_mode=pl.Buffered(lhs_buffers)))
        args.append(l)
    row_off = 0
    for l in lhs_list:
        k = l.shape[1]
        in_specs.append(pl.BlockSpec((None, k, tn),
                                     functools.partial(lambda i, j, rb: (layer, rb, j), rb=row_off // k)))
        args.append(w)
        row_off += k
    if row_ss is not None:
        in_specs.append(pl.BlockSpec((tm, LANES), lambda i, j: (i, 0)))
        args.append(row_ss)
    if residual is not None:
        in_specs.append(pl.BlockSpec((tm, tn), lambda i, j: (i, j)))
        args.append(residual)
    out_specs = [pl.BlockSpec((tm, tn), lambda i, j: (i, j))]
    out_shape = [jax.ShapeDtypeStruct((m, n), F32)]
    if next_gain is not None:
        in_specs.append(pl.BlockSpec((1, tn), lambda i, j: (0, j)))
        args.append(next_gain.reshape(1, n))
        out_specs += [pl.BlockSpec((tm, tn), lambda i, j: (i, j)), pl.BlockSpec((tm, LANES), lambda i, j: (i, 0))]
        out_shape += [jax.ShapeDtypeStruct((m, n), BF16), jax.ShapeDtypeStruct((m, LANES), F32)]
    outs = pl.pallas_call(
        functools.partial(_mm_kernel, n_lhs=len(lhs_list), epilogue=epilogue, tm=tm,
                          scale_rows=row_ss is not None, emit_norm=next_gain is not None),
        grid=(m // tm, n // tn),
        in_specs=in_specs,
        out_specs=out_specs,
        out_shape=out_shape,
        scratch_shapes=[pltpu.VMEM((tm, LANES), F32)] if row_ss is not None else [],
        compiler_params=_params("parallel", "arbitrary"),
        name=name,
    )(*args)
    return outs if next_gain is not None else outs[0]


def _causal_conv3(x, prev, w, seq_len):
    rows, c = x.shape
    nseq = rows // seq_len
    t = lax.broadcasted_iota(jnp.int32, (rows, 1), 0) & (seq_len - 1)
    p0 = jnp.broadcast_to(prev[:, 0:1, :], (nseq, seq_len, c)).reshape(rows, c)
    p1 = jnp.broadcast_to(prev[:, 1:2, :], (nseq, seq_len, c)).reshape(rows, c)
    x1 = jnp.where(t == 0, p1, pltpu.roll(x, 1, 0))
    x2 = jnp.where(t == 0, p0, jnp.where(t == 1, p1, pltpu.roll(x, 2, 0)))
    return x2 * w[0:1, :] + x1 * w[1:2, :] + x * w[2:3, :]


def _last_two_rows(x, seq_len):
    rows, c = x.shape
    return x.reshape(rows // seq_len, seq_len, c)[:, seq_len - 2:, :]


def _conv_a_kernel(gb_ref, gc_ref, hv_ref, w_ref, prev_ref, a_ref, st_ref, *, seq_len):
    xin = gc_ref[...] * hv_ref[...]
    conv = _causal_conv3(xin, prev_ref[...], w_ref[...], seq_len)
    a_ref[...] = (gb_ref[...] * conv).astype(a_ref.dtype)
    st_ref[...] = _last_two_rows(xin, seq_len)


def _conv_a(proj, conv_w, prev, seq_len, tc=256):
    m = proj.shape[0]
    nseq_tile = ROWS // seq_len
    nj = CONV_DIM // tc
    return pl.pallas_call(
        functools.partial(_conv_a_kernel, seq_len=seq_len),
        grid=(m // ROWS, nj),
        in_specs=[pl.BlockSpec((ROWS, tc), lambda i, j: (i, j)),
                  pl.BlockSpec((ROWS, tc), lambda i, j: (i, j + nj)),
                  pl.BlockSpec((ROWS, tc), lambda i, j: (i, j + 2 * nj)),
                  pl.BlockSpec((3, tc), lambda i, j: (0, j)),
                  pl.BlockSpec((nseq_tile, 2, tc), lambda i, j: (i, 0, j))],
        out_specs=[pl.BlockSpec((ROWS, tc), lambda i, j: (i, j)),
                   pl.BlockSpec((nseq_tile, 2, tc), lambda i, j: (i, 0, j))],
        out_shape=[jax.ShapeDtypeStruct((m, CONV_DIM), BF16),
                   jax.ShapeDtypeStruct((m // seq_len, 2, CONV_DIM), F32)],
        compiler_params=_params("parallel", "parallel"),
        name="conv_a",
    )(proj, proj, proj, conv_w, prev)


def _alibi_slopes():
    exps = np.arange(1, N_HEADS + 1, dtype=np.float32) * (8.0 / N_HEADS)
    return np.power(2.0, -exps).astype(np.float32)


def _attn_kernel(sink_ref, q_ref, kh_ref, kn_ref, vh_ref, vn_ref, o_ref, o_acc, bias_scr, sc_scr, w_scr,
                 *, tq, band_mask):
    nk = kh_ref.shape[0] + kn_ref.shape[0]
    group = N_HEADS // N_KV
    rows = group * tq
    slopes = _alibi_slopes()
    n_var = bias_scr.shape[0]

    @pl.when((pl.program_id(0) == 0) & (pl.program_id(1) == 0))
    def _():
        t = lax.broadcasted_iota(jnp.int32, (tq, nk), 0)
        s = lax.broadcasted_iota(jnp.int32, (tq, nk), 1)
        neg_dist = -jnp.abs(t + (WINDOW - s)).astype(F32)
        q_chunk, k_chunk = t // CHUNK, s // CHUNK
        for var in range(n_var):
            for h in range(N_HEADS):
                bias = float(slopes[h]) * neg_dist
                if band_mask:
                    first_valid = WINDOW // CHUNK if var == 1 else 0
                    visible = ((k_chunk >= q_chunk) & (k_chunk <= q_chunk + WINDOW // CHUNK)
                               & (k_chunk >= first_valid))
                    bias = jnp.where(visible, bias, NEG_INF)
                bias_scr[var, h] = bias

    var = jnp.where(pl.program_id(1) == 0, n_var - 1, 0)
    rc = min(tq, ATTN_ROWS)

    for kh in range(N_KV):
        cols = slice(kh * HEAD_DIM, (kh + 1) * HEAD_DIM)
        k_all = jnp.concatenate([kh_ref[:, cols], kn_ref[:, cols]], axis=0).astype(BF16)
        q = jnp.concatenate([q_ref[:, h * HEAD_DIM:(h + 1) * HEAD_DIM]
                             for h in range(kh * group, (kh + 1) * group)], axis=0)
        q = (q * (HEAD_DIM ** -0.5)).astype(BF16)
        sc_scr[kh] = lax.dot_general(q, k_all, (((1,), (1,)), ((), ())), preferred_element_type=F32)
    for h in range(N_HEADS):
        kh, g = divmod(h, group)
        sink = sink_ref[h]
        for r0 in range(0, tq, rc):
            rws = slice(g * tq + r0, g * tq + r0 + rc)
            s = sc_scr[kh, rws, :] + bias_scr[var, h, r0:r0 + rc, :]
            m = jnp.maximum(jnp.max(s, axis=-1, keepdims=True), sink)
            p = jnp.exp(s - m)
            denom = jnp.sum(p, axis=-1, keepdims=True) + jnp.exp(sink - m)
            w_scr[kh, rws, :] = (p * (1.0 / denom)).astype(BF16)
    for kh in range(N_KV):
        cols = slice(kh * HEAD_DIM, (kh + 1) * HEAD_DIM)
        v_all = jnp.concatenate([vh_ref[:, cols], vn_ref[:, cols]], axis=0).astype(BF16)
        out = _dot(w_scr[kh], v_all)
        for g in range(group):
            h = kh * group + g
            o_acc[:, h * HEAD_DIM:(h + 1) * HEAD_DIM] = out[g * tq:(g + 1) * tq, :]
    o_ref[...] = o_acc[...].astype(o_ref.dtype)


def _attention(q_src, q_map, k_hist, v_hist, hist_maps, k_new, v_new, new_maps, sinks, *, tq, band_mask, grid):
    m = q_src.shape[0]
    nblk = grid[1]
    return pl.pallas_call(
        functools.partial(_attn_kernel, tq=tq, band_mask=band_mask),
        grid=grid,
        in_specs=[pl.BlockSpec(memory_space=pltpu.SMEM),
                  pl.BlockSpec((tq, Q_DIM), q_map),
                  pl.BlockSpec((WINDOW, KV_DIM), hist_maps[0]), pl.BlockSpec((tq, KV_DIM), new_maps[0]),
                  pl.BlockSpec((WINDOW, KV_DIM), hist_maps[1]), pl.BlockSpec((tq, KV_DIM), new_maps[1])],
        out_specs=pl.BlockSpec((tq, Q_DIM), lambda b, c: (b * nblk + c, 0)),
        out_shape=jax.ShapeDtypeStruct((m, Q_DIM), BF16),
        scratch_shapes=[pltpu.VMEM((tq, Q_DIM), F32),
                        pltpu.VMEM((2 if band_mask else 1, N_HEADS, tq, WINDOW + tq), F32),
                        pltpu.VMEM((N_KV, N_HEADS // N_KV * tq, WINDOW + tq), F32),
                        pltpu.VMEM((N_KV, N_HEADS // N_KV * tq, WINDOW + tq), BF16)],
        compiler_params=_params("arbitrary", "arbitrary"),
        name="swa",
    )(sinks, q_src, k_hist, k_new, v_hist, v_new)


def _gmlp_kernel(u_ref, v_ref, lng_ref, lnb_ref, ws_ref, bs_ref, o_ref, *maybe_vn_ref, cl):
    v = v_ref[...]
    mu = jnp.mean(v, axis=-1, keepdims=True)
    var = jnp.mean(jnp.square(v - mu), axis=-1, keepdims=True)
    vn = (v - mu) * lax.rsqrt(var + LN_EPS) * lng_ref[...] + lnb_ref[...]
    for vn_ref in maybe_vn_ref:
        vn_ref[...] = vn
    row = lax.broadcasted_iota(jnp.int32, (cl, cl), 0)
    col = lax.broadcasted_iota(jnp.int32, (cl, cl), 1)
    lower = row >= col
    for g in range(GMLP_GROUPS):
        cols = slice(g * GMLP_GDIM, (g + 1) * GMLP_GDIM)
        ws = jnp.where(lower, ws_ref[g, 0:cl, 0:cl], 0.0).astype(BF16)
        s = _dot(ws, vn[:, cols].astype(BF16)) + bs_ref[0:cl, g:g + 1]
        o_ref[:, cols] = (u_ref[:, cols] * s).astype(o_ref.dtype)


def _gmlp_gate(z, ln_g, ln_b, w_s, b_s_t, cl, want_vn):
    m = z.shape[0]
    n_out = 2 if want_vn else 1
    return pl.pallas_call(
        functools.partial(_gmlp_kernel, cl=cl),
        grid=(m // cl,),
        in_specs=[pl.BlockSpec((cl, D_MODEL), lambda i: (i, 0)),
                  pl.BlockSpec((cl, D_MODEL), lambda i: (i, 1)),
                  pl.BlockSpec((1, D_MODEL), lambda i: (0, 0)),
                  pl.BlockSpec((1, D_MODEL), lambda i: (0, 0)),
                  pl.BlockSpec((GMLP_GROUPS, GMLP_CHUNK, GMLP_CHUNK), lambda i: (0, 0, 0)),
                  pl.BlockSpec((GMLP_CHUNK, GMLP_GROUPS), lambda i: (0, 0))],
        out_specs=[pl.BlockSpec((cl, D_MODEL), lambda i: (i, 0)),
                   pl.BlockSpec((cl, D_MODEL), lambda i: (i, 0))][:n_out],
        out_shape=[jax.ShapeDtypeStruct((m, D_MODEL), BF16),
                   jax.ShapeDtypeStruct((m, D_MODEL), F32)][:n_out],
        compiler_params=_params("parallel"),
        name="gmlp_gate",
    )(z, z, ln_g.reshape(1, -1), ln_b.reshape(1, -1), w_s, b_s_t)


def _ffn_up_kernel(h_ref, wg_ref, wu_ref, cwg_ref, cwu_ref, bg_ref, bu_ref, pg_ref, pu_ref,
                   act_ref, sg_ref, su_ref, ug_scr, uu_scr, *, seq_len):
    wg, wu = _as_bf16(wg_ref[...]), _as_bf16(wu_ref[...])
    ug_scr[:, 0:SCR_PITCH * HALO, :] = jnp.zeros((N_SLAB, SCR_PITCH * HALO, LANES), F32)
    uu_scr[:, 0:SCR_PITCH * HALO, :] = jnp.zeros((N_SLAB, SCR_PITCH * HALO, LANES), F32)
    row = lax.broadcasted_iota(jnp.int32, (EP_ROWS, 1), 0)

    def rows_at(scr, s, t, n):
        return scr[s, pl.ds(SCR_PITCH * (HALO + t), n, stride=SCR_PITCH), :]

    def conv(scr, prev_ref, cw_ref, b_ref, s, start, z):
        lanes = slice(s * LANES, (s + 1) * LANES)
        x0, x1, x2 = (rows_at(scr, s, start - back, EP_ROWS) for back in range(3))
        if start % seq_len == 0:
            prev = prev_ref[start // seq_len]
            x1 = jnp.where(row == 0, prev[1:2, lanes], x1)
            x2 = jnp.where(row == 0, prev[0:1, lanes], jnp.where(row == 1, prev[1:2, lanes], x2))

        def tap(x, w):
            return (x.reshape(EP_ROWS // 8, 8, LANES) * (w + z)[None]).reshape(EP_ROWS, LANES)

        return (tap(x2, cw_ref[0:1, lanes]) + tap(x1, cw_ref[1:2, lanes]) + tap(x0, cw_ref[2:3, lanes])
                + b_ref[:, lanes])

    zs = [jnp.zeros((8, LANES), F32)] * N_SLAB

    row0 = 0
    for sub_rows in FFN_SUB_TILES:
        h = h_ref[row0:row0 + sub_rows, :]
        up_g, up_u = _dot(h, wg), _dot(h, wu)
        for s in range(N_SLAB):
            dst = pl.ds(SCR_PITCH * (HALO + row0), sub_rows, stride=SCR_PITCH)
            ug_scr[s, dst, :] = up_g[:, s * LANES:(s + 1) * LANES]
            uu_scr[s, dst, :] = up_u[:, s * LANES:(s + 1) * LANES]
        chunk_starts = range(row0, row0 + sub_rows, EP_ROWS)
        row0 += sub_rows
        for start in chunk_starts:
            end = start + EP_ROWS
            for s in range(N_SLAB):
                lanes = slice(s * LANES, (s + 1) * LANES)
                cg = conv(ug_scr, pg_ref, cwg_ref, bg_ref, s, start, zs[s])
                cu = conv(uu_scr, pu_ref, cwu_ref, bu_ref, s, start, zs[s])
                act = cg * jax.nn.sigmoid(cg) * cu
                act_ref[start:end, lanes] = act.astype(act_ref.dtype)
                bits = lax.bitcast_convert_type(act[0:8, :], jnp.uint32)
                zs[s] = ((bits >> 16) >> 16).astype(F32)
                if end % seq_len == 0:
                    sg_ref[end // seq_len - 1, :, lanes] = rows_at(ug_scr, s, end - 2, 2)
                    su_ref[end // seq_len - 1, :, lanes] = rows_at(uu_scr, s, end - 2, 2)


def _ffn_up(h, w_up, conv_w, conv_b, layer, prev, prev_layer, seq_len):
    m = h.shape[0]
    nseq_tile = ROWS // seq_len
    nj = N_FF_TILES
    st_spec_g = pl.BlockSpec((None, nseq_tile, 2, FF_COLS), lambda i, j: (prev_layer, i, 0, j))
    st_spec_u = pl.BlockSpec((None, nseq_tile, 2, FF_COLS), lambda i, j: (prev_layer, i, 0, j + nj))
    conv_b = conv_b.reshape(conv_b.shape[0], 1, -1)
    return pl.pallas_call(
        functools.partial(_ffn_up_kernel, seq_len=seq_len),
        grid=(m // ROWS, nj),
        in_specs=[pl.BlockSpec((ROWS, D_MODEL), lambda i, j: (i, 0), pipeline_mode=pl.Buffered(1)),
                  pl.BlockSpec((None, D_MODEL, FF_COLS), lambda i, j: (layer, 0, j)),
                  pl.BlockSpec((None, D_MODEL, FF_COLS), lambda i, j: (layer, 0, j + nj)),
                  pl.BlockSpec((None, 3, FF_COLS), lambda i, j: (layer, 0, j)),
                  pl.BlockSpec((None, 3, FF_COLS), lambda i, j: (layer, 0, j + nj)),
                  pl.BlockSpec((None, 1, FF_COLS), lambda i, j: (layer, 0, j)),
                  pl.BlockSpec((None, 1, FF_COLS), lambda i, j: (layer, 0, j + nj)),
                  st_spec_g, st_spec_u],
        out_specs=[pl.BlockSpec((ROWS, FF_COLS), lambda i, j: (i, j)),
                   pl.BlockSpec((nseq_tile, 2, FF_COLS), lambda i, j: (i, 0, j)),
                   pl.BlockSpec((nseq_tile, 2, FF_COLS), lambda i, j: (i, 0, j))],
        out_shape=[jax.ShapeDtypeStruct((m, D_FF), BF16),
                   jax.ShapeDtypeStruct((m // seq_len, 2, D_FF), F32),
                   jax.ShapeDtypeStruct((m // seq_len, 2, D_FF), F32)],
        scratch_shapes=[pltpu.VMEM((N_SLAB, SCR_PITCH * (HALO + ROWS), LANES), F32)] * 2,
        compiler_params=_params("parallel", "arbitrary"),
        name="ffn_up",
    )(h, w_up, w_up, conv_w, conv_w, conv_b, conv_b, prev, prev)


def _trunk(x3, conv_a_cache, swa_k_cache, swa_v_cache, ffn_cache, wts):
    bsz, seq_len, _ = x3.shape
    m = bsz * seq_len
    sample = ffn_cache is not None
    x = x3.reshape(m, D_MODEL)

    h = _rmsnorm(x, wts["norm_mix"][0], BF16)
    proj = _matmul([h], wts["w_in_ab"], 0, **MM_F32W, name="in_ab")
    prev_a = conv_a_cache[0] if sample else jnp.zeros((bsz, 2, CONV_DIM), F32)
    a, conv_state = _conv_a(proj, wts["conv_w_a"][0], prev_a, seq_len)

    q_col = 3 * CONV_DIM // Q_DIM
    k_col = (3 * CONV_DIM + Q_DIM) // KV_DIM
    v_col = k_col + 1
    if sample:
        k_hist = swa_k_cache[0].reshape(bsz * WINDOW, KV_DIM)
        v_hist = swa_v_cache[0].reshape(bsz * WINDOW, KV_DIM)
        hist = [lambda b, c: (b, 0)] * 2
        new = [lambda b, c: (b, k_col), lambda b, c: (b, v_col)]
        attn = _attention(proj, lambda b, c: (b, q_col), k_hist, v_hist, hist, proj, proj, new,
                          wts["sinks"][0], tq=seq_len, band_mask=False, grid=(bsz, 1))
    else:
        nblk = seq_len // WINDOW
        def blk(back, col):
            return lambda b, c: (b * nblk + jnp.maximum(c - back, 0), col)
        attn = _attention(proj, blk(0, q_col), proj, proj, [blk(1, k_col), blk(1, v_col)], proj, proj,
                          [blk(0, k_col), blk(0, v_col)], wts["sinks"][0], tq=WINDOW, band_mask=True,
                          grid=(bsz, nblk))

    k_new = proj[:, k_col * KV_DIM:(k_col + 1) * KV_DIM].reshape(bsz, seq_len, N_KV, HEAD_DIM)
    v_new = proj[:, v_col * KV_DIM:(v_col + 1) * KV_DIM].reshape(bsz, seq_len, N_KV, HEAD_DIM)
    if sample:
        k_state = jnp.concatenate([swa_k_cache[0], k_new], axis=1)[:, seq_len:]
        v_state = jnp.concatenate([swa_v_cache[0], v_new], axis=1)[:, seq_len:]
    else:
        k_state, v_state = k_new[:, seq_len - WINDOW:], v_new[:, seq_len - WINDOW:]

    x = _matmul([a, attn], wts["w_out_ab"], 0, **MM_F32W, epilogue="residual", residual=x,
                name="out_ab")

    ffn_states = []

    def conv_ffn(x, layer, next_gain):
        h = _rmsnorm(x, wts["norm_ffn"][layer], BF16)
        prev, prev_layer = (ffn_cache, layer) if sample else (jnp.zeros((1, bsz, 2, 2 * D_FF), F32), 0)
        act, st_g, st_u = _ffn_up(h, wts["w_up_ffn"], wts["conv_w_ffn"], wts["conv_b_ffn"], layer,
                                  prev, prev_layer, seq_len)
        ffn_states.append(jnp.concatenate([st_g, st_u], axis=-1))
        return _matmul([act], wts["w_down_ffn"], layer, **MM_DOWN, epilogue="residual",
                       residual=x, next_gain=next_gain, name="ffn_down")

    x, xg, ss = conv_ffn(x, 0, wts["norm_mix"][1])

    z = _matmul([xg], wts["w_in_c"], 0, **MM_F32W, epilogue="gelu", row_ss=ss, name="in_c")
    cl = min(seq_len, GMLP_CHUNK)
    gated, *maybe_vn = _gmlp_gate(z, wts["ln_g_c"][0], wts["ln_b_c"][0], wts["w_s_c"][0], wts["b_s_c"][0].T,
                                  cl, want_vn=sample)
    gv_state = maybe_vn[0].reshape(1, bsz, seq_len, D_MODEL) if sample else None
    x = _matmul([gated], wts["w_out_c"], 0, **MM_F32W, epilogue="residual", residual=x,
                name="out_c")
    x = conv_ffn(x, 1, None)

    y = _rmsnorm(x, wts["norm_final"], F32)
    return (y.reshape(bsz, seq_len, D_MODEL), conv_state[None], k_state[None], v_state[None],
            jnp.stack(ffn_states), gv_state)


def kernel(x_prompt, x_sample, state_conv_a, cache_swa_k, cache_swa_v, state_ffn_conv, norm_mix, norm_ffn,
           norm_final, w_in_ab, conv_w_a, sinks, w_out_ab, w_in_c, ln_g_c, ln_b_c, w_s_c, b_s_c, w_out_c,
           w_up_ffn, conv_w_ffn, conv_b_ffn, w_down_ffn):
    wts = dict(norm_mix=norm_mix, norm_ffn=norm_ffn, norm_final=norm_final,
               w_in_ab=w_in_ab, conv_w_a=conv_w_a, sinks=sinks, w_out_ab=w_out_ab,
               w_in_c=w_in_c, ln_g_c=ln_g_c, ln_b_c=ln_b_c, w_s_c=w_s_c, b_s_c=b_s_c,
               w_out_c=w_out_c, w_up_ffn=w_up_ffn, conv_w_ffn=conv_w_ffn,
               conv_b_ffn=conv_b_ffn, w_down_ffn=w_down_ffn.astype(BF16))
    y_p, conv_p, k_p, v_p, ffn_p, _ = _trunk(x_prompt, None, None, None, None, wts)
    y_s, conv_s, k_s, v_s, ffn_s, gv_s = _trunk(x_sample, state_conv_a, cache_swa_k, cache_swa_v,
                                                state_ffn_conv, wts)
    return (y_p, y_s, conv_p, conv_s, k_p, k_s, v_p, v_s, ffn_p, ffn_s, gv_s)
```

```python
import functools

import numpy as np
import jax
import jax.numpy as jnp
from jax import lax
from jax.experimental import pallas as pl
from jax.experimental.pallas import tpu as pltpu

F32 = jnp.float32
BF16 = jnp.bfloat16

D_MODEL = 4096
CHUNK = 64
CONV_DIM = 2048
N_HEADS = 32
N_KV = 8
HEAD_DIM = 64
Q_DIM = N_HEADS * HEAD_DIM
KV_DIM = N_KV * HEAD_DIM
WINDOW = 128
IN_AB_DIM = 3 * CONV_DIM + Q_DIM + 2 * KV_DIM
GMLP_CHUNK = 128
GMLP_GROUPS = 16
GMLP_GDIM = D_MODEL // GMLP_GROUPS
D_FF = 11008
RMS_EPS = 1e-6
LN_EPS = 1e-5
NEG_INF = -1e30

V7X_VMEM_BYTES = 64 * 1024 * 1024
VMEM_LIMIT = V7X_VMEM_BYTES - 8 * 1024 * 1024

ROWS = 2048
SUB_ROWS = 512
ATTN_ROWS = 128
EP_ROWS = 32
FFN_SUB_TILES = (512, 512, 512, 320, 192)
HALO = 8
FF_COLS = 256
N_FF_TILES = D_FF // FF_COLS
LANES = 128
N_SLAB = FF_COLS // LANES
SCR_PITCH = 2
MM_F32W = dict(tm=2048, tn=512, lhs_buffers=1)
MM_DOWN = dict(tm=512, tn=512, lhs_buffers=2)


def _params(*sem):
    return pltpu.CompilerParams(dimension_semantics=sem, vmem_limit_bytes=VMEM_LIMIT)


def _dot(a, b):
    return jnp.dot(a, b, preferred_element_type=F32)


def _as_bf16(x):
    return x if x.dtype == BF16 else x.astype(BF16)


def _rmsnorm_kernel(x_ref, g_ref, o_ref):
    x = x_ref[...]
    r = lax.rsqrt(jnp.mean(x * x, axis=-1, keepdims=True) + RMS_EPS)
    o_ref[...] = (x * r * g_ref[...]).astype(o_ref.dtype)


def _rmsnorm(x, g, out_dtype, tm=512):
    m, d = x.shape
    return pl.pallas_call(
        _rmsnorm_kernel,
        grid=(m // tm,),
        in_specs=[pl.BlockSpec((tm, d), lambda i: (i, 0)),
                  pl.BlockSpec((1, d), lambda i: (0, 0))],
        out_specs=pl.BlockSpec((tm, d), lambda i: (i, 0)),
        out_shape=jax.ShapeDtypeStruct((m, d), out_dtype),
        compiler_params=_params("parallel"),
        name="rmsnorm",
    )(x, g.reshape(1, d))


def _gelu(x):
    return 0.5 * x * (1.0 + lax.erf(x * np.float32(np.sqrt(0.5))))


def _row_scale(ss):
    ms = jnp.sum(ss, axis=-1, keepdims=True) * (1.0 / D_MODEL)
    return jnp.broadcast_to(lax.rsqrt(ms + RMS_EPS), ss.shape)


def _per_lane_tile(fn, x):
    return jnp.concatenate([fn(x[:, k * LANES:(k + 1) * LANES]) for k in range(x.shape[1] // LANES)], axis=1)


def _lane_partial_sumsq(x):
    sq = x * x
    part = sq[:, 0:LANES]
    for k in range(1, sq.shape[1] // LANES):
        part = part + sq[:, k * LANES:(k + 1) * LANES]
    return part


def _mm_kernel(*refs, n_lhs, epilogue, tm, scale_rows, emit_norm):
    refs = list(refs)
    lhs, w_refs = refs[:n_lhs], refs[n_lhs:2 * n_lhs]
    rest = refs[2 * n_lhs:]
    ss_in = rest.pop(0) if scale_rows else None
    res_ref = rest.pop(0) if epilogue == "residual" else None
    gain_ref = rest.pop(0) if emit_norm else None
    o_ref = rest.pop(0)
    xg_ref, ss_out = (rest.pop(0), rest.pop(0)) if emit_norm else (None, None)
    r_scr = rest.pop(0) if scale_rows else None
    first_col = pl.program_id(1) == 0

    if scale_rows:
        @pl.when(first_col)
        def _():
            r_scr[...] = _row_scale(ss_in[...])
    if emit_norm:
        @pl.when(first_col)
        def _():
            ss_out[...] = jnp.zeros(ss_out.shape, F32)

    ws = [_as_bf16(w[...]) for w in w_refs]
    for r in range(tm // SUB_ROWS):
        rows = slice(r * SUB_ROWS, (r + 1) * SUB_ROWS)
        acc = _dot(lhs[0][rows, :], ws[0])
        for l in range(1, n_lhs):
            acc = acc + _dot(lhs[l][rows, :], ws[l])
        if scale_rows:
            scale = r_scr[rows, :]
            acc = _per_lane_tile(lambda t: t * scale, acc)
        if epilogue == "residual":
            acc = res_ref[rows, :] + acc
        elif epilogue == "gelu":
            acc = _gelu(acc)
        o_ref[rows, :] = acc
        if emit_norm:
            xg_ref[rows, :] = (acc * gain_ref[...]).astype(BF16)
            ss_out[rows, :] += _lane_partial_sumsq(acc)


def _matmul(lhs_list, w, layer, *, tm, tn, lhs_buffers, epilogue="none", residual=None, row_ss=None,
            next_gain=None, name):
    m = lhs_list[0].shape[0]
    n = w.shape[2]
    in_specs, args = [], []
    for l in lhs_list:
        in_specs.append(pl.BlockSpec((tm, l.shape[1]), lambda i, j: (i, 0),
                                     pipeline_mode=pl.Buffered(lhs_buffers)))
        args.append(l)
    row_off = 0
    for l in lhs_list:
        k = l.shape[1]
        in_specs.append(pl.BlockSpec((None, k, tn),
                                     functools.partial(lambda i, j, rb: (layer, rb, j), rb=row_off // k)))
        args.append(w)
        row_off += k
    if row_ss is not None:
        in_specs.append(pl.BlockSpec((tm, LANES), lambda i, j: (i, 0)))
        args.append(row_ss)
    if residual is not None:
        in_specs.append(pl.BlockSpec((tm, tn), lambda i, j: (i, j)))
        args.append(residual)
    out_specs = [pl.BlockSpec((tm, tn), lambda i, j: (i, j))]
    out_shape = [jax.ShapeDtypeStruct((m, n), F32)]
    if next_gain is not None:
        in_specs.append(pl.BlockSpec((1, tn), lambda i, j: (0, j)))
        args.append(next_gain.reshape(1, n))
        out_specs += [pl.BlockSpec((tm, tn), lambda i, j: (i, j)), pl.BlockSpec((tm, LANES), lambda i, j: (i, 0))]
        out_shape += [jax.ShapeDtypeStruct((m, n), BF16), jax.ShapeDtypeStruct((m, LANES), F32)]
    outs = pl.pallas_call(
        functools.partial(_mm_kernel, n_lhs=len(lhs_list), epilogue=epilogue, tm=tm,
                          scale_rows=row_ss is not None, emit_norm=next_gain is not None),
        grid=(m // tm, n // tn),
        in_specs=in_specs,
        out_specs=out_specs,
        out_shape=out_shape,
        scratch_shapes=[pltpu.VMEM((tm, LANES), F32)] if row_ss is not None else [],
        compiler_params=_params("parallel", "arbitrary"),
        name=name,
    )(*args)
    return outs if next_gain is not None else outs[0]


def _causal_conv3(x, prev, w, seq_len):
    rows, c = x.shape
    nseq = rows // seq_len
    t = lax.broadcasted_iota(jnp.int32, (rows, 1), 0) & (seq_len - 1)
    p0 = jnp.broadcast_to(prev[:, 0:1, :], (nseq, seq_len, c)).reshape(rows, c)
    p1 = jnp.broadcast_to(prev[:, 1:2, :], (nseq, seq_len, c)).reshape(rows, c)
    x1 = jnp.where(t == 0, p1, pltpu.roll(x, 1, 0))
    x2 = jnp.where(t == 0, p0, jnp.where(t == 1, p1, pltpu.roll(x, 2, 0)))
    return x2 * w[0:1, :] + x1 * w[1:2, :] + x * w[2:3, :]


def _last_two_rows(x, seq_len):
    rows, c = x.shape
    return x.reshape(rows // seq_len, seq_len, c)[:, seq_len - 2:, :]


def _conv_a_kernel(gb_ref, gc_ref, hv_ref, w_ref, prev_ref, a_ref, st_ref, *, seq_len):
    xin = gc_ref[...] * hv_ref[...]
    conv = _causal_conv3(xin, prev_ref[...], w_ref[...], seq_len)
    a_ref[...] = (gb_ref[...] * conv).astype(a_ref.dtype)
    st_ref[...] = _last_two_rows(xin, seq_len)


def _conv_a(proj, conv_w, prev, seq_len, tc=256):
    m = proj.shape[0]
    nseq_tile = ROWS // seq_len
    nj = CONV_DIM // tc
    return pl.pallas_call(
        functools.partial(_conv_a_kernel, seq_len=seq_len),
        grid=(m // ROWS, nj),
        in_specs=[pl.BlockSpec((ROWS, tc), lambda i, j: (i, j)),
                  pl.BlockSpec((ROWS, tc), lambda i, j: (i, j + nj)),
                  pl.BlockSpec((ROWS, tc), lambda i, j: (i, j + 2 * nj)),
                  pl.BlockSpec((3, tc), lambda i, j: (0, j)),
                  pl.BlockSpec((nseq_tile, 2, tc), lambda i, j: (i, 0, j))],
        out_specs=[pl.BlockSpec((ROWS, tc), lambda i, j: (i, j)),
                   pl.BlockSpec((nseq_tile, 2, tc), lambda i, j: (i, 0, j))],
        out_shape=[jax.ShapeDtypeStruct((m, CONV_DIM), BF16),
                   jax.ShapeDtypeStruct((m // seq_len, 2, CONV_DIM), F32)],
        compiler_params=_params("parallel", "parallel"),
        name="conv_a",
    )(proj, proj, proj, conv_w, prev)


def _alibi_slopes():
    exps = np.arange(1, N_HEADS + 1, dtype=np.float32) * (8.0 / N_HEADS)
    return np.power(2.0, -exps).astype(np.float32)


def _attn_kernel(sink_ref, q_ref, kh_ref, kn_ref, vh_ref, vn_ref, o_ref, o_acc, bias_scr, sc_scr, w_scr,
                 *, tq, band_mask):
    nk = kh_ref.shape[0] + kn_ref.shape[0]
    group = N_HEADS // N_KV
    rows = group * tq
    slopes = _alibi_slopes()
    n_var = bias_scr.shape[0]

    def per_head(shape, head_in_group, vals):
        out = jnp.full(shape, vals[group - 1], F32)
        for g in range(group - 2, -1, -1):
            out = jnp.where(head_in_group == g, vals[g], out)
        return out

    @pl.when((pl.program_id(0) == 0) & (pl.program_id(1) == 0))
    def _():
        s = lax.broadcasted_iota(jnp.int32, (nk, rows), 0)
        col = lax.broadcasted_iota(jnp.int32, (nk, rows), 1)
        t = col & (tq - 1)
        neg_dist = -jnp.abs(t + (WINDOW - s)).astype(F32)
        q_chunk, k_chunk = t // CHUNK, s // CHUNK
        for var in range(n_var):
            for kh in range(N_KV):
                bias = per_head((nk, rows), col // tq, [float(slopes[kh * group + g]) for g in range(group)])
                bias = bias * neg_dist
                if band_mask:
                    first_valid = WINDOW // CHUNK if var == 1 else 0
                    visible = ((k_chunk >= q_chunk) & (k_chunk <= q_chunk + WINDOW // CHUNK)
                               & (k_chunk >= first_valid))
                    bias = jnp.where(visible, bias, NEG_INF)
                bias_scr[var, kh] = bias

    var = jnp.where(pl.program_id(1) == 0, n_var - 1, 0)
    head_of_lane = lax.broadcasted_iota(jnp.int32, (1, rows), 1) // tq

    for kh in range(N_KV):
        cols = slice(kh * HEAD_DIM, (kh + 1) * HEAD_DIM)
        k_all = jnp.concatenate([kh_ref[:, cols], kn_ref[:, cols]], axis=0).astype(BF16)
        q = jnp.concatenate([q_ref[:, h * HEAD_DIM:(h + 1) * HEAD_DIM]
                             for h in range(kh * group, (kh + 1) * group)], axis=0)
        q = (q * (HEAD_DIM ** -0.5)).astype(BF16)
        sc_scr[kh] = lax.dot_general(k_all, q, (((1,), (1,)), ((), ())), preferred_element_type=F32)
    for kh in range(N_KV):
        sink_row = per_head((1, rows), head_of_lane, [sink_ref[kh * group + g] for g in range(group)])
        for c0 in range(0, rows, LANES):
            lanes = slice(c0, c0 + LANES)
            s = sc_scr[kh, :, lanes] + bias_scr[var, kh, :, lanes]
            sink = sink_row[:, lanes]
            m = jnp.maximum(jnp.max(s, axis=0, keepdims=True), sink)
            p = jnp.exp(s - m)
            denom = jnp.sum(p, axis=0, keepdims=True) + jnp.exp(sink - m)
            w_scr[kh, :, lanes] = (p * (1.0 / denom)).astype(BF16)
    for kh in range(N_KV):
        cols = slice(kh * HEAD_DIM, (kh + 1) * HEAD_DIM)
        v_all = jnp.concatenate([vh_ref[:, cols], vn_ref[:, cols]], axis=0).astype(BF16)
        out = lax.dot_general(w_scr[kh], v_all, (((0,), (0,)), ((), ())), preferred_element_type=F32)
        for g in range(group):
            h = kh * group + g
            o_acc[:, h * HEAD_DIM:(h + 1) * HEAD_DIM] = out[g * tq:(g + 1) * tq, :]
    o_ref[...] = o_acc[...].astype(o_ref.dtype)


def _attention(q_src, q_map, k_hist, v_hist, hist_maps, k_new, v_new, new_maps, sinks, *, tq, band_mask, grid):
    m = q_src.shape[0]
    nblk = grid[1]
    return pl.pallas_call(
        functools.partial(_attn_kernel, tq=tq, band_mask=band_mask),
        grid=grid,
        in_specs=[pl.BlockSpec(memory_space=pltpu.SMEM),
                  pl.BlockSpec((tq, Q_DIM), q_map),
                  pl.BlockSpec((WINDOW, KV_DIM), hist_maps[0]), pl.BlockSpec((tq, KV_DIM), new_maps[0]),
                  pl.BlockSpec((WINDOW, KV_DIM), hist_maps[1]), pl.BlockSpec((tq, KV_DIM), new_maps[1])],
        out_specs=pl.BlockSpec((tq, Q_DIM), lambda b, c: (b * nblk + c, 0)),
        out_shape=jax.ShapeDtypeStruct((m, Q_DIM), BF16),
        scratch_shapes=[pltpu.VMEM((tq, Q_DIM), F32),
                        pltpu.VMEM((2 if band_mask else 1, N_KV, WINDOW + tq, N_HEADS // N_KV * tq), F32),
                        pltpu.VMEM((N_KV, WINDOW + tq, N_HEADS // N_KV * tq), F32),
                        pltpu.VMEM((N_KV, WINDOW + tq, N_HEADS // N_KV * tq), BF16)],
        compiler_params=_params("arbitrary", "arbitrary"),
        name="swa",
    )(sinks, q_src, k_hist, k_new, v_hist, v_new)


def _gmlp_kernel(u_ref, v_ref, lng_ref, lnb_ref, ws_ref, bs_ref, o_ref, *maybe_vn_ref, cl):
    v = v_ref[...]
    mu = jnp.mean(v, axis=-1, keepdims=True)
    var = jnp.mean(jnp.square(v - mu), axis=-1, keepdims=True)
    vn = (v - mu) * lax.rsqrt(var + LN_EPS) * lng_ref[...] + lnb_ref[...]
    for vn_ref in maybe_vn_ref:
        vn_ref[...] = vn
    row = lax.broadcasted_iota(jnp.int32, (cl, cl), 0)
    col = lax.broadcasted_iota(jnp.int32, (cl, cl), 1)
    lower = row >= col
    for g in range(GMLP_GROUPS):
        cols = slice(g * GMLP_GDIM, (g + 1) * GMLP_GDIM)
        ws = jnp.where(lower, ws_ref[g, 0:cl, 0:cl], 0.0).astype(BF16)
        s = _dot(ws, vn[:, cols].astype(BF16)) + bs_ref[0:cl, g:g + 1]
        o_ref[:, cols] = (u_ref[:, cols] * s).astype(o_ref.dtype)


def _gmlp_gate(z, ln_g, ln_b, w_s, b_s_t, cl, want_vn):
    m = z.shape[0]
    n_out = 2 if want_vn else 1
    return pl.pallas_call(
        functools.partial(_gmlp_kernel, cl=cl),
        grid=(m // cl,),
        in_specs=[pl.BlockSpec((cl, D_MODEL), lambda i: (i, 0)),
                  pl.BlockSpec((cl, D_MODEL), lambda i: (i, 1)),
                  pl.BlockSpec((1, D_MODEL), lambda i: (0, 0)),
                  pl.BlockSpec((1, D_MODEL), lambda i: (0, 0)),
                  pl.BlockSpec((GMLP_GROUPS, GMLP_CHUNK, GMLP_CHUNK), lambda i: (0, 0, 0)),
                  pl.BlockSpec((GMLP_CHUNK, GMLP_GROUPS), lambda i: (0, 0))],
        out_specs=[pl.BlockSpec((cl, D_MODEL), lambda i: (i, 0)),
                   pl.BlockSpec((cl, D_MODEL), lambda i: (i, 0))][:n_out],
        out_shape=[jax.ShapeDtypeStruct((m, D_MODEL), BF16),
                   jax.ShapeDtypeStruct((m, D_MODEL), F32)][:n_out],
        compiler_params=_params("parallel"),
        name="gmlp_gate",
    )(z, z, ln_g.reshape(1, -1), ln_b.reshape(1, -1), w_s, b_s_t)


def _ffn_up_kernel(h_ref, wg_ref, wu_ref, cwg_ref, cwu_ref, bg_ref, bu_ref, pg_ref, pu_ref,
                   act_ref, sg_ref, su_ref, ug_scr, uu_scr, *, seq_len):
    wg, wu = _as_bf16(wg_ref[...]), _as_bf16(wu_ref[...])
    ug_scr[:, 0:SCR_PITCH * HALO, :] = jnp.zeros((N_SLAB, SCR_PITCH * HALO, LANES), F32)
    uu_scr[:, 0:SCR_PITCH * HALO, :] = jnp.zeros((N_SLAB, SCR_PITCH * HALO, LANES), F32)
    row = lax.broadcasted_iota(jnp.int32, (EP_ROWS, 1), 0)

    def rows_at(scr, s, t, n):
        return scr[s, pl.ds(SCR_PITCH * (HALO + t), n, stride=SCR_PITCH), :]

    def conv(scr, prev_ref, cw_ref, b_ref, s, start, z):
        lanes = slice(s * LANES, (s + 1) * LANES)
        x0, x1, x2 = (rows_at(scr, s, start - back, EP_ROWS) for back in range(3))
        if start % seq_len == 0:
            prev = prev_ref[start // seq_len]
            x1 = jnp.where(row == 0, prev[1:2, lanes], x1)
            x2 = jnp.where(row == 0, prev[0:1, lanes], jnp.where(row == 1, prev[1:2, lanes], x2))

        def tap(x, w):
            return (x.reshape(EP_ROWS // 8, 8, LANES) * (w + z)[None]).reshape(EP_ROWS, LANES)

        return (tap(x2, cw_ref[0:1, lanes]) + tap(x1, cw_ref[1:2, lanes]) + tap(x0, cw_ref[2:3, lanes])
                + b_ref[:, lanes])

    zs = [jnp.zeros((8, LANES), F32)] * N_SLAB

    row0 = 0
    for sub_rows in FFN_SUB_TILES:
        h = h_ref[row0:row0 + sub_rows, :]
        up_g, up_u = _dot(h, wg), _dot(h, wu)
        for s in range(N_SLAB):
            dst = pl.ds(SCR_PITCH * (HALO + row0), sub_rows, stride=SCR_PITCH)
            ug_scr[s, dst, :] = up_g[:, s * LANES:(s + 1) * LANES]
            uu_scr[s, dst, :] = up_u[:, s * LANES:(s + 1) * LANES]
        chunk_starts = range(row0, row0 + sub_rows, EP_ROWS)
        row0 += sub_rows
        for start in chunk_starts:
            end = start + EP_ROWS
            for s in range(N_SLAB):
                lanes = slice(s * LANES, (s + 1) * LANES)
                cg = conv(ug_scr, pg_ref, cwg_ref, bg_ref, s, start, zs[s])
                cu = conv(uu_scr, pu_ref, cwu_ref, bu_ref, s, start, zs[s])
                act = cg * jax.nn.sigmoid(cg) * cu
                act_ref[start:end, lanes] = act.astype(act_ref.dtype)
                bits = lax.bitcast_convert_type(act[0:8, :], jnp.uint32)
                zs[s] = ((bits >> 16) >> 16).astype(F32)
                if end % seq_len == 0:
                    sg_ref[end // seq_len - 1, :, lanes] = rows_at(ug_scr, s, end - 2, 2)
                    su_ref[end // seq_len - 1, :, lanes] = rows_at(uu_scr, s, end - 2, 2)


def _ffn_up(h, w_up, conv_w, conv_b, layer, prev, prev_layer, seq_len):
    m = h.shape[0]
    nseq_tile = ROWS // seq_len
    nj = N_FF_TILES
    st_spec_g = pl.BlockSpec((None, nseq_tile, 2, FF_COLS), lambda i, j: (prev_layer, i, 0, j))
    st_spec_u = pl.BlockSpec((None, nseq_tile, 2, FF_COLS), lambda i, j: (prev_layer, i, 0, j + nj))
    conv_b = conv_b.reshape(conv_b.shape[0], 1, -1)
    return pl.pallas_call(
        functools.partial(_ffn_up_kernel, seq_len=seq_len),
        grid=(m // ROWS, nj),
        in_specs=[pl.BlockSpec((ROWS, D_MODEL), lambda i, j: (i, 0), pipeline_mode=pl.Buffered(1)),
                  pl.BlockSpec((None, D_MODEL, FF_COLS), lambda i, j: (layer, 0, j)),
                  pl.BlockSpec((None, D_MODEL, FF_COLS), lambda i, j: (layer, 0, j + nj)),
                  pl.BlockSpec((None, 3, FF_COLS), lambda i, j: (layer, 0, j)),
                  pl.BlockSpec((None, 3, FF_COLS), lambda i, j: (layer, 0, j + nj)),
                  pl.BlockSpec((None, 1, FF_COLS), lambda i, j: (layer, 0, j)),
                  pl.BlockSpec((None, 1, FF_COLS), lambda i, j: (layer, 0, j + nj)),
                  st_spec_g, st_spec_u],
        out_specs=[pl.BlockSpec((ROWS, FF_COLS), lambda i, j: (i, j)),
                   pl.BlockSpec((nseq_tile, 2, FF_COLS), lambda i, j: (i, 0, j)),
                   pl.BlockSpec((nseq_tile, 2, FF_COLS), lambda i, j: (i, 0, j))],
        out_shape=[jax.ShapeDtypeStruct((m, D_FF), BF16),
                   jax.ShapeDtypeStruct((m // seq_len, 2, D_FF), F32),
                   jax.ShapeDtypeStruct((m // seq_len, 2, D_FF), F32)],
        scratch_shapes=[pltpu.VMEM((N_SLAB, SCR_PITCH * (HALO + ROWS), LANES), F32)] * 2,
        compiler_params=_params("parallel", "arbitrary"),
        name="ffn_up",
    )(h, w_up, w_up, conv_w, conv_w, conv_b, conv_b, prev, prev)


def _trunk(x3, conv_a_cache, swa_k_cache, swa_v_cache, ffn_cache, wts):
    bsz, seq_len, _ = x3.shape
    m = bsz * seq_len
    sample = ffn_cache is not None
    x = x3.reshape(m, D_MODEL)

    h = _rmsnorm(x, wts["norm_mix"][0], BF16)
    proj = _matmul([h], wts["w_in_ab"], 0, **MM_F32W, name="in_ab")
    prev_a = conv_a_cache[0] if sample else jnp.zeros((bsz, 2, CONV_DIM), F32)
    a, conv_state = _conv_a(proj, wts["conv_w_a"][0], prev_a, seq_len)

    q_col = 3 * CONV_DIM // Q_DIM
    k_col = (3 * CONV_DIM + Q_DIM) // KV_DIM
    v_col = k_col + 1
    if sample:
        k_hist = swa_k_cache[0].reshape(bsz * WINDOW, KV_DIM)
        v_hist = swa_v_cache[0].reshape(bsz * WINDOW, KV_DIM)
        hist = [lambda b, c: (b, 0)] * 2
        new = [lambda b, c: (b, k_col), lambda b, c: (b, v_col)]
        attn = _attention(proj, lambda b, c: (b, q_col), k_hist, v_hist, hist, proj, proj, new,
                          wts["sinks"][0], tq=seq_len, band_mask=False, grid=(bsz, 1))
    else:
        nblk = seq_len // WINDOW
        def blk(back, col):
            return lambda b, c: (b * nblk + jnp.maximum(c - back, 0), col)
        attn = _attention(proj, blk(0, q_col), proj, proj, [blk(1, k_col), blk(1, v_col)], proj, proj,
                          [blk(0, k_col), blk(0, v_col)], wts["sinks"][0], tq=WINDOW, band_mask=True,
                          grid=(bsz, nblk))

    k_new = proj[:, k_col * KV_DIM:(k_col + 1) * KV_DIM].reshape(bsz, seq_len, N_KV, HEAD_DIM)
    v_new = proj[:, v_col * KV_DIM:(v_col + 1) * KV_DIM].reshape(bsz, seq_len, N_KV, HEAD_DIM)
    if sample:
        k_state = jnp.concatenate([swa_k_cache[0], k_new], axis=1)[:, seq_len:]
        v_state = jnp.concatenate([swa_v_cache[0], v_new], axis=1)[:, seq_len:]
    else:
        k_state, v_state = k_new[:, seq_len - WINDOW:], v_new[:, seq_len - WINDOW:]

    x = _matmul([a, attn], wts["w_out_ab"], 0, **MM_F32W, epilogue="residual", residual=x,
                name="out_ab")

    ffn_states = []

    def conv_ffn(x, layer, next_gain):
        h = _rmsnorm(x, wts["norm_ffn"][layer], BF16)
        prev, prev_layer = (ffn_cache, layer) if sample else (jnp.zeros((1, bsz, 2, 2 * D_FF), F32), 0)
        act, st_g, st_u = _ffn_up(h, wts["w_up_ffn"], wts["conv_w_ffn"], wts["conv_b_ffn"], layer,
                                  prev, prev_layer, seq_len)
        ffn_states.append(jnp.concatenate([st_g, st_u], axis=-1))
        return _matmul([act], wts["w_down_ffn"], layer, **MM_DOWN, epilogue="residual",
                       residual=x, next_gain=next_gain, name="ffn_down")

    x, xg, ss = conv_ffn(x, 0, wts["norm_mix"][1])

    z = _matmul([xg], wts["w_in_c"], 0, **MM_F32W, epilogue="gelu", row_ss=ss, name="in_c")
    cl = min(seq_len, GMLP_CHUNK)
    gated, *maybe_vn = _gmlp_gate(z, wts["ln_g_c"][0], wts["ln_b_c"][0], wts["w_s_c"][0], wts["b_s_c"][0].T,
                                  cl, want_vn=sample)
    gv_state = maybe_vn[0].reshape(1, bsz, seq_len, D_MODEL) if sample else None
    x = _matmul([gated], wts["w_out_c"], 0, **MM_F32W, epilogue="residual", residual=x,
                name="out_c")
    x = conv_ffn(x, 1, None)

    y = _rmsnorm(x, wts["norm_final"], F32)
    return (y.reshape(bsz, seq_len, D_MODEL), conv_state[None], k_state[None], v_state[None],
            jnp.stack(ffn_states), gv_state)


def kernel(x_prompt, x_sample, state_conv_a, cache_swa_k, cache_swa_v, state_ffn_conv, norm_mix, norm_ffn,
           norm_final, w_in_ab, conv_w_a, sinks, w_out_ab, w_in_c, ln_g_c, ln_b_c, w_s_c, b_s_c, w_out_c,
           w_up_ffn, conv_w_ffn, conv_b_ffn, w_down_ffn):
    wts = dict(norm_mix=norm_mix, norm_ffn=norm_ffn, norm_final=norm_final,
               w_in_ab=w_in_ab, conv_w_a=conv_w_a, sinks=sinks, w_out_ab=w_out_ab,
               w_in_c=w_in_c, ln_g_c=ln_g_c, ln_b_c=ln_b_c, w_s_c=w_s_c, b_s_c=b_s_c,
               w_out_c=w_out_c, w_up_ffn=w_up_ffn, conv_w_ffn=conv_w_ffn,
               conv_b_ffn=conv_b_ffn, w_down_ffn=w_down_ffn.astype(BF16))
    y_p, conv_p, k_p, v_p, ffn_p, _ = _trunk(x_prompt, None, None, None, None, wts)
    y_s, conv_s, k_s, v_s, ffn_s, gv_s = _trunk(x_sample, state_conv_a, cache_swa_k, cache_swa_v,
                                                state_ffn_conv, wts)
    return (y_p, y_s, conv_p, conv_s, k_p, k_s, v_p, v_s, ffn_p, ffn_s, gv_s)
```

```python
import functools

import numpy as np
import jax
import jax.numpy as jnp
from jax import lax
from jax.experimental import pallas as pl
from jax.experimental.pallas import tpu as pltpu

F32 = jnp.float32
BF16 = jnp.bfloat16

D_MODEL = 4096
CHUNK = 64
CONV_DIM = 2048
N_HEADS = 32
N_KV = 8
HEAD_DIM = 64
Q_DIM = N_HEADS * HEAD_DIM
KV_DIM = N_KV * HEAD_DIM
WINDOW = 128
GMLP_CHUNK = 128
GMLP_GROUPS = 16
GMLP_GDIM = D_MODEL // GMLP_GROUPS
D_FF = 11008
RMS_EPS = 1e-6
LN_EPS = 1e-5
NEG_INF = -1e30

V7X_VMEM_BYTES = 64 * 1024 * 1024
VMEM_LIMIT = V7X_VMEM_BYTES - 8 * 1024 * 1024

ROWS = 2048
SUB_ROWS = 512
EP_ROWS = 32
FFN_SUB_TILES = (512, 512, 512, 320, 192)
HALO = 8
FF_COLS = 256
N_FF_TILES = D_FF // FF_COLS
LANES = 128
SUBLANES = 8
N_SLAB = FF_COLS // LANES
SCR_PITCH = 2
MM_F32W = dict(tm=2048, tn=512, lhs_buffers=1)
MM_DOWN = dict(tm=512, tn=512, lhs_buffers=2)


def _params(*sem):
    return pltpu.CompilerParams(dimension_semantics=sem, vmem_limit_bytes=VMEM_LIMIT)


def _dot(a, b):
    return jnp.dot(a, b, preferred_element_type=F32)


def _as_bf16(x):
    return x if x.dtype == BF16 else x.astype(BF16)


def _rmsnorm_kernel(x_ref, g_ref, o_ref):
    x = x_ref[...]
    r = lax.rsqrt(jnp.mean(x * x, axis=-1, keepdims=True) + RMS_EPS)
    o_ref[...] = (x * r * g_ref[...]).astype(o_ref.dtype)


def _rmsnorm(x, g, out_dtype, tm=512):
    m, d = x.shape
    return pl.pallas_call(
        _rmsnorm_kernel,
        grid=(m // tm,),
        in_specs=[pl.BlockSpec((tm, d), lambda i: (i, 0)),
                  pl.BlockSpec((1, d), lambda i: (0, 0))],
        out_specs=pl.BlockSpec((tm, d), lambda i: (i, 0)),
        out_shape=jax.ShapeDtypeStruct((m, d), out_dtype),
        compiler_params=_params("parallel"),
        name="rmsnorm",
    )(x, g.reshape(1, d))


def _gelu(x):
    return 0.5 * x * (1.0 + lax.erf(x * np.float32(np.sqrt(0.5))))


def _row_scale(ss):
    ms = jnp.sum(ss, axis=-1, keepdims=True) * (1.0 / D_MODEL)
    return jnp.broadcast_to(lax.rsqrt(ms + RMS_EPS), ss.shape)


def _per_lane_tile(fn, x):
    return jnp.concatenate([fn(x[:, k * LANES:(k + 1) * LANES]) for k in range(x.shape[1] // LANES)], axis=1)


def _lane_partial_sumsq(x):
    sq = x * x
    part = sq[:, 0:LANES]
    for k in range(1, sq.shape[1] // LANES):
        part = part + sq[:, k * LANES:(k + 1) * LANES]
    return part


def _mm_kernel(*refs, n_lhs, epilogue, tm, scale_rows, emit_norm):
    refs = list(refs)
    lhs, w_refs = refs[:n_lhs], refs[n_lhs:2 * n_lhs]
    rest = refs[2 * n_lhs:]
    ss_in = rest.pop(0) if scale_rows else None
    res_ref = rest.pop(0) if epilogue == "residual" else None
    gain_ref = rest.pop(0) if emit_norm else None
    o_ref = rest.pop(0)
    xg_ref, ss_out = (rest.pop(0), rest.pop(0)) if emit_norm else (None, None)
    r_scr = rest.pop(0) if scale_rows else None
    first_col = pl.program_id(1) == 0

    if scale_rows:
        @pl.when(first_col)
        def _():
            r_scr[...] = _row_scale(ss_in[...])
    if emit_norm:
        @pl.when(first_col)
        def _():
            ss_out[...] = jnp.zeros(ss_out.shape, F32)

    ws = [_as_bf16(w[...]) for w in w_refs]
    for r in range(tm // SUB_ROWS):
        rows = slice(r * SUB_ROWS, (r + 1) * SUB_ROWS)
        acc = _dot(lhs[0][rows, :], ws[0])
        for l in range(1, n_lhs):
            acc = acc + _dot(lhs[l][rows, :], ws[l])
        if scale_rows:
            scale = r_scr[rows, :]
            acc = _per_lane_tile(lambda t: t * scale, acc)
        if epilogue == "residual":
            acc = res_ref[rows, :] + acc
        elif epilogue == "gelu":
            acc = _gelu(acc)
        o_ref[rows, :] = acc
        if emit_norm:
            xg_ref[rows, :] = (acc * gain_ref[...]).astype(BF16)
            ss_out[rows, :] += _lane_partial_sumsq(acc)


def _matmul(lhs_list, w, layer, *, tm, tn, lhs_buffers, epilogue="none", residual=None, row_ss=None,
            next_gain=None, name):
    m = lhs_list[0].shape[0]
    n = w.shape[2]
    in_specs, args = [], []
    for l in lhs_list:
        in_specs.append(pl.BlockSpec((tm, l.shape[1]), lambda i, j: (i, 0),
                                     pipeline_mode=pl.Buffered(lhs_buffers)))
        args.append(l)
    row_off = 0
    for l in lhs_list:
        k = l.shape[1]
        in_specs.append(pl.BlockSpec((None, k, tn),
                                     functools.partial(lambda i, j, rb: (layer, rb, j), rb=row_off // k)))
        args.append(w)
        row_off += k
    if row_ss is not None:
        in_specs.append(pl.BlockSpec((tm, LANES), lambda i, j: (i, 0)))
        args.append(row_ss)
    if residual is not None:
        in_specs.append(pl.BlockSpec((tm, tn), lambda i, j: (i, j)))
        args.append(residual)
    out_specs = [pl.BlockSpec((tm, tn), lambda i, j: (i, j))]
    out_shape = [jax.ShapeDtypeStruct((m, n), F32)]
    if next_gain is not None:
        in_specs.append(pl.BlockSpec((1, tn), lambda i, j: (0, j)))
        args.append(next_gain.reshape(1, n))
        out_specs += [pl.BlockSpec((tm, tn), lambda i, j: (i, j)), pl.BlockSpec((tm, LANES), lambda i, j: (i, 0))]
        out_shape += [jax.ShapeDtypeStruct((m, n), BF16), jax.ShapeDtypeStruct((m, LANES), F32)]
    outs = pl.pallas_call(
        functools.partial(_mm_kernel, n_lhs=len(lhs_list), epilogue=epilogue, tm=tm,
                          scale_rows=row_ss is not None, emit_norm=next_gain is not None),
        grid=(m // tm, n // tn),
        in_specs=in_specs,
        out_specs=out_specs,
        out_shape=out_shape,
        scratch_shapes=[pltpu.VMEM((tm, LANES), F32)] if row_ss is not None else [],
        compiler_params=_params("parallel", "arbitrary"),
        name=name,
    )(*args)
    return outs if next_gain is not None else outs[0]


def _causal_conv3(x, prev, w, seq_len):
    rows, c = x.shape
    nseq = rows // seq_len
    t = lax.broadcasted_iota(jnp.int32, (rows, 1), 0) & (seq_len - 1)
    p0 = jnp.broadcast_to(prev[:, 0:1, :], (nseq, seq_len, c)).reshape(rows, c)
    p1 = jnp.broadcast_to(prev[:, 1:2, :], (nseq, seq_len, c)).reshape(rows, c)
    x1 = jnp.where(t == 0, p1, pltpu.roll(x, 1, 0))
    x2 = jnp.where(t == 0, p0, jnp.where(t == 1, p1, pltpu.roll(x, 2, 0)))
    return x2 * w[0:1, :] + x1 * w[1:2, :] + x * w[2:3, :]


def _last_two_rows(x, seq_len):
    rows, c = x.shape
    return x.reshape(rows // seq_len, seq_len, c)[:, seq_len - 2:, :]


def _conv_a_kernel(gb_ref, gc_ref, hv_ref, w_ref, prev_ref, a_ref, st_ref, *, seq_len):
    xin = gc_ref[...] * hv_ref[...]
    conv = _causal_conv3(xin, prev_ref[...], w_ref[...], seq_len)
    a_ref[...] = (gb_ref[...] * conv).astype(a_ref.dtype)
    st_ref[...] = _last_two_rows(xin, seq_len)


def _conv_a(proj, conv_w, prev, seq_len, tc=256):
    m = proj.shape[0]
    nseq_tile = ROWS // seq_len
    nj = CONV_DIM // tc
    return pl.pallas_call(
        functools.partial(_conv_a_kernel, seq_len=seq_len),
        grid=(m // ROWS, nj),
        in_specs=[pl.BlockSpec((ROWS, tc), lambda i, j: (i, j)),
                  pl.BlockSpec((ROWS, tc), lambda i, j: (i, j + nj)),
                  pl.BlockSpec((ROWS, tc), lambda i, j: (i, j + 2 * nj)),
                  pl.BlockSpec((3, tc), lambda i, j: (0, j)),
                  pl.BlockSpec((nseq_tile, 2, tc), lambda i, j: (i, 0, j))],
        out_specs=[pl.BlockSpec((ROWS, tc), lambda i, j: (i, j)),
                   pl.BlockSpec((nseq_tile, 2, tc), lambda i, j: (i, 0, j))],
        out_shape=[jax.ShapeDtypeStruct((m, CONV_DIM), BF16),
                   jax.ShapeDtypeStruct((m // seq_len, 2, CONV_DIM), F32)],
        compiler_params=_params("parallel", "parallel"),
        name="conv_a",
    )(proj, proj, proj, conv_w, prev)


def _alibi_slopes():
    exps = np.arange(1, N_HEADS + 1, dtype=np.float32) * (8.0 / N_HEADS)
    return np.power(2.0, -exps).astype(np.float32)


def _attn_kernel(sink_ref, q_ref, kh_ref, kn_ref, vh_ref, vn_ref, o_ref, o_acc, bias_scr, sc_scr, w_scr,
                 *maybe_cap_scr, tq):
    nk = kh_ref.shape[0] + kn_ref.shape[0]
    group = N_HEADS // N_KV
    rows = group * tq
    slopes = _alibi_slopes()

    def per_head(shape, head_in_group, vals):
        out = jnp.full(shape, vals[group - 1], F32)
        for g in range(group - 2, -1, -1):
            out = jnp.where(head_in_group == g, vals[g], out)
        return out

    @pl.when((pl.program_id(0) == 0) & (pl.program_id(1) == 0))
    def _():
        s = lax.broadcasted_iota(jnp.int32, (nk, rows), 0)
        col = lax.broadcasted_iota(jnp.int32, (nk, rows), 1)
        t = col & (tq - 1)
        neg_dist = -jnp.abs(t + (WINDOW - s)).astype(F32)
        for kh in range(N_KV):
            slope = per_head((nk, rows), col // tq, [float(slopes[kh * group + g]) for g in range(group)])
            bias_scr[kh] = slope * neg_dist
        for cap_scr in maybe_cap_scr:
            q_chunk, k_chunk = t // CHUNK, s // CHUNK
            in_band = (k_chunk >= q_chunk) & (k_chunk <= q_chunk + WINDOW // CHUNK)
            for first_block in range(2):
                visible = in_band & (k_chunk >= first_block * (WINDOW // CHUNK))
                cap_scr[first_block] = jnp.where(visible, jnp.inf, NEG_INF)

    first_block = jnp.where(pl.program_id(1) == 0, 1, 0)
    head_of_lane = lax.broadcasted_iota(jnp.int32, (1, rows), 1) // tq

    for kh in range(N_KV):
        cols = slice(kh * HEAD_DIM, (kh + 1) * HEAD_DIM)
        k_all = jnp.concatenate([kh_ref[:, cols], kn_ref[:, cols]], axis=0).astype(BF16)
        q = jnp.concatenate([q_ref[:, h * HEAD_DIM:(h + 1) * HEAD_DIM]
                             for h in range(kh * group, (kh + 1) * group)], axis=0)
        q = (q * (HEAD_DIM ** -0.5)).astype(BF16)
        sc_scr[kh] = lax.dot_general(k_all, q, (((1,), (1,)), ((), ())), preferred_element_type=F32)
    for kh in range(N_KV):
        sink_row = per_head((1, rows), head_of_lane, [sink_ref[kh * group + g] for g in range(group)])
        for c0 in range(0, rows, LANES):
            lanes = slice(c0, c0 + LANES)
            s = sc_scr[kh, :, lanes] + bias_scr[kh, :, lanes]
            for cap_scr in maybe_cap_scr:
                s = jnp.minimum(s, cap_scr[first_block, :, lanes])
            sink = sink_row[:, lanes]
            m = jnp.maximum(jnp.max(s, axis=0, keepdims=True), sink)
            p = jnp.exp(s - m)
            denom = jnp.sum(p, axis=0, keepdims=True) + jnp.exp(sink - m)
            w_scr[kh, :, lanes] = (p * (1.0 / denom)).astype(BF16)
    for kh in range(N_KV):
        cols = slice(kh * HEAD_DIM, (kh + 1) * HEAD_DIM)
        v_all = jnp.concatenate([vh_ref[:, cols], vn_ref[:, cols]], axis=0).astype(BF16)
        out = lax.dot_general(w_scr[kh], v_all, (((0,), (0,)), ((), ())), preferred_element_type=F32)
        for g in range(group):
            h = kh * group + g
            o_acc[:, h * HEAD_DIM:(h + 1) * HEAD_DIM] = out[g * tq:(g + 1) * tq, :]
    o_ref[...] = o_acc[...].astype(o_ref.dtype)


def _attention(q_src, q_map, k_hist, v_hist, hist_maps, k_new, v_new, new_maps, sinks, *, tq, band_mask, grid):
    m = q_src.shape[0]
    nblk = grid[1]
    nk, rows = WINDOW + tq, N_HEADS // N_KV * tq
    return pl.pallas_call(
        functools.partial(_attn_kernel, tq=tq),
        grid=grid,
        in_specs=[pl.BlockSpec(memory_space=pltpu.SMEM),
                  pl.BlockSpec((tq, Q_DIM), q_map),
                  pl.BlockSpec((WINDOW, KV_DIM), hist_maps[0]), pl.BlockSpec((tq, KV_DIM), new_maps[0]),
                  pl.BlockSpec((WINDOW, KV_DIM), hist_maps[1]), pl.BlockSpec((tq, KV_DIM), new_maps[1])],
        out_specs=pl.BlockSpec((tq, Q_DIM), lambda b, c: (b * nblk + c, 0)),
        out_shape=jax.ShapeDtypeStruct((m, Q_DIM), BF16),
        scratch_shapes=[pltpu.VMEM((tq, Q_DIM), F32)]
                       + [pltpu.VMEM((N_KV, nk, rows), F32)] * 2
                       + [pltpu.VMEM((N_KV, nk, rows), BF16)]
                       + ([pltpu.VMEM((2, nk, rows), F32)] if band_mask else []),
        compiler_params=_params("arbitrary", "arbitrary"),
        name="swa",
    )(sinks, q_src, k_hist, k_new, v_hist, v_new)


def _gmlp_kernel(u_ref, v_ref, lng_ref, lnb_ref, ws_ref, bs_ref, o_ref, *maybe_vn_ref, cl):
    v = v_ref[...]
    mu = jnp.mean(v, axis=-1, keepdims=True)
    var = jnp.mean(jnp.square(v - mu), axis=-1, keepdims=True)
    vn = (v - mu) * lax.rsqrt(var + LN_EPS) * lng_ref[...] + lnb_ref[...]
    for vn_ref in maybe_vn_ref:
        vn_ref[...] = vn
    row = lax.broadcasted_iota(jnp.int32, (cl, cl), 0)
    col = lax.broadcasted_iota(jnp.int32, (cl, cl), 1)
    lower = row >= col
    for g in range(GMLP_GROUPS):
        cols = slice(g * GMLP_GDIM, (g + 1) * GMLP_GDIM)
        ws = jnp.where(lower, ws_ref[g, 0:cl, 0:cl], 0.0).astype(BF16)
        s = _dot(ws, vn[:, cols].astype(BF16)) + bs_ref[0:cl, g:g + 1]
        o_ref[:, cols] = (u_ref[:, cols] * s).astype(o_ref.dtype)


def _gmlp_gate(z, ln_g, ln_b, w_s, b_s_t, cl, want_vn):
    m = z.shape[0]
    n_out = 2 if want_vn else 1
    return pl.pallas_call(
        functools.partial(_gmlp_kernel, cl=cl),
        grid=(m // cl,),
        in_specs=[pl.BlockSpec((cl, D_MODEL), lambda i: (i, 0)),
                  pl.BlockSpec((cl, D_MODEL), lambda i: (i, 1)),
                  pl.BlockSpec((1, D_MODEL), lambda i: (0, 0)),
                  pl.BlockSpec((1, D_MODEL), lambda i: (0, 0)),
                  pl.BlockSpec((GMLP_GROUPS, GMLP_CHUNK, GMLP_CHUNK), lambda i: (0, 0, 0)),
                  pl.BlockSpec((GMLP_CHUNK, GMLP_GROUPS), lambda i: (0, 0))],
        out_specs=[pl.BlockSpec((cl, D_MODEL), lambda i: (i, 0)),
                   pl.BlockSpec((cl, D_MODEL), lambda i: (i, 0))][:n_out],
        out_shape=[jax.ShapeDtypeStruct((m, D_MODEL), BF16),
                   jax.ShapeDtypeStruct((m, D_MODEL), F32)][:n_out],
        compiler_params=_params("parallel"),
        name="gmlp_gate",
    )(z, z, ln_g.reshape(1, -1), ln_b.reshape(1, -1), w_s, b_s_t)


def _ffn_up_kernel(h_ref, wg_ref, wu_ref, cwg_ref, cwu_ref, bg_ref, bu_ref, pg_ref, pu_ref,
                   act_ref, sg_ref, su_ref, ug_scr, uu_scr, *, seq_len):
    wg, wu = _as_bf16(wg_ref[...]), _as_bf16(wu_ref[...])
    ug_scr[:, 0:SCR_PITCH * HALO, :] = jnp.zeros((N_SLAB, SCR_PITCH * HALO, LANES), F32)
    uu_scr[:, 0:SCR_PITCH * HALO, :] = jnp.zeros((N_SLAB, SCR_PITCH * HALO, LANES), F32)
    row = lax.broadcasted_iota(jnp.int32, (EP_ROWS, 1), 0)

    def rows_at(scr, s, t, n):
        return scr[s, pl.ds(SCR_PITCH * (HALO + t), n, stride=SCR_PITCH), :]

    def conv(scr, prev_ref, cw_ref, b_ref, s, start, z):
        lanes = slice(s * LANES, (s + 1) * LANES)
        x0, x1, x2 = (rows_at(scr, s, start - back, EP_ROWS) for back in range(3))
        if start % seq_len == 0:
            prev = prev_ref[start // seq_len]
            x1 = jnp.where(row == 0, prev[1:2, lanes], x1)
            x2 = jnp.where(row == 0, prev[0:1, lanes], jnp.where(row == 1, prev[1:2, lanes], x2))

        def tap(x, w):
            return (x.reshape(EP_ROWS // SUBLANES, SUBLANES, LANES) * (w + z)[None]).reshape(EP_ROWS, LANES)

        return (tap(x2, cw_ref[0:1, lanes]) + tap(x1, cw_ref[1:2, lanes]) + tap(x0, cw_ref[2:3, lanes])
                + b_ref[:, lanes])

    zs = [jnp.zeros((SUBLANES, LANES), F32)] * N_SLAB

    row0 = 0
    for sub_rows in FFN_SUB_TILES:
        h = h_ref[row0:row0 + sub_rows, :]
        up_g, up_u = _dot(h, wg), _dot(h, wu)
        for s in range(N_SLAB):
            dst = pl.ds(SCR_PITCH * (HALO + row0), sub_rows, stride=SCR_PITCH)
            ug_scr[s, dst, :] = up_g[:, s * LANES:(s + 1) * LANES]
            uu_scr[s, dst, :] = up_u[:, s * LANES:(s + 1) * LANES]
        chunk_starts = range(row0, row0 + sub_rows, EP_ROWS)
        row0 += sub_rows
        for start in chunk_starts:
            end = start + EP_ROWS
            for s in range(N_SLAB):
                lanes = slice(s * LANES, (s + 1) * LANES)
                cg = conv(ug_scr, pg_ref, cwg_ref, bg_ref, s, start, zs[s])
                cu = conv(uu_scr, pu_ref, cwu_ref, bu_ref, s, start, zs[s])
                act = cg * jax.nn.sigmoid(cg) * cu
                act_ref[start:end, lanes] = act.astype(act_ref.dtype)
                bits = lax.bitcast_convert_type(act[0:SUBLANES, :], jnp.uint32)
                zs[s] = ((bits >> 16) >> 16).astype(F32)
                if end % seq_len == 0:
                    sg_ref[end // seq_len - 1, :, lanes] = rows_at(ug_scr, s, end - 2, 2)
                    su_ref[end // seq_len - 1, :, lanes] = rows_at(uu_scr, s, end - 2, 2)


def _ffn_up(h, w_up, conv_w, conv_b, layer, prev, prev_layer, seq_len):
    m = h.shape[0]
    assert sum(FFN_SUB_TILES) == ROWS and all(t % EP_ROWS == 0 for t in FFN_SUB_TILES)
    assert min(seq_len, ROWS) % EP_ROWS == 0 and ROWS % seq_len == 0
    nseq_tile = ROWS // seq_len
    nj = N_FF_TILES
    st_spec_g = pl.BlockSpec((None, nseq_tile, 2, FF_COLS), lambda i, j: (prev_layer, i, 0, j))
    st_spec_u = pl.BlockSpec((None, nseq_tile, 2, FF_COLS), lambda i, j: (prev_layer, i, 0, j + nj))
    conv_b = conv_b.reshape(conv_b.shape[0], 1, -1)
    return pl.pallas_call(
        functools.partial(_ffn_up_kernel, seq_len=seq_len),
        grid=(m // ROWS, nj),
        in_specs=[pl.BlockSpec((ROWS, D_MODEL), lambda i, j: (i, 0), pipeline_mode=pl.Buffered(1)),
                  pl.BlockSpec((None, D_MODEL, FF_COLS), lambda i, j: (layer, 0, j)),
                  pl.BlockSpec((None, D_MODEL, FF_COLS), lambda i, j: (layer, 0, j + nj)),
                  pl.BlockSpec((None, 3, FF_COLS), lambda i, j: (layer, 0, j)),
                  pl.BlockSpec((None, 3, FF_COLS), lambda i, j: (layer, 0, j + nj)),
                  pl.BlockSpec((None, 1, FF_COLS), lambda i, j: (layer, 0, j)),
                  pl.BlockSpec((None, 1, FF_COLS), lambda i, j: (layer, 0, j + nj)),
                  st_spec_g, st_spec_u],
        out_specs=[pl.BlockSpec((ROWS, FF_COLS), lambda i, j: (i, j)),
                   pl.BlockSpec((nseq_tile, 2, FF_COLS), lambda i, j: (i, 0, j)),
                   pl.BlockSpec((nseq_tile, 2, FF_COLS), lambda i, j: (i, 0, j))],
        out_shape=[jax.ShapeDtypeStruct((m, D_FF), BF16),
                   jax.ShapeDtypeStruct((m // seq_len, 2, D_FF), F32),
                   jax.ShapeDtypeStruct((m // seq_len, 2, D_FF), F32)],
        scratch_shapes=[pltpu.VMEM((N_SLAB, SCR_PITCH * (HALO + ROWS), LANES), F32)] * 2,
        compiler_params=_params("parallel", "arbitrary"),
        name="ffn_up",
    )(h, w_up, w_up, conv_w, conv_w, conv_b, conv_b, prev, prev)


def _trunk(x3, conv_a_cache, swa_k_cache, swa_v_cache, ffn_cache, wts):
    bsz, seq_len, _ = x3.shape
    m = bsz * seq_len
    sample = ffn_cache is not None
    x = x3.reshape(m, D_MODEL)

    h = _rmsnorm(x, wts["norm_mix"][0], BF16)
    proj = _matmul([h], wts["w_in_ab"], 0, **MM_F32W, name="in_ab")
    prev_a = conv_a_cache[0] if sample else jnp.zeros((bsz, 2, CONV_DIM), F32)
    a, conv_state = _conv_a(proj, wts["conv_w_a"][0], prev_a, seq_len)

    q_col = 3 * CONV_DIM // Q_DIM
    k_col = (3 * CONV_DIM + Q_DIM) // KV_DIM
    v_col = k_col + 1
    if sample:
        k_hist = swa_k_cache[0].reshape(bsz * WINDOW, KV_DIM)
        v_hist = swa_v_cache[0].reshape(bsz * WINDOW, KV_DIM)
        hist = [lambda b, c: (b, 0)] * 2
        new = [lambda b, c: (b, k_col), lambda b, c: (b, v_col)]
        attn = _attention(proj, lambda b, c: (b, q_col), k_hist, v_hist, hist, proj, proj, new,
                          wts["sinks"][0], tq=seq_len, band_mask=False, grid=(bsz, 1))
    else:
        nblk = seq_len // WINDOW
        def blk(back, col):
            return lambda b, c: (b * nblk + jnp.maximum(c - back, 0), col)
        attn = _attention(proj, blk(0, q_col), proj, proj, [blk(1, k_col), blk(1, v_col)], proj, proj,
                          [blk(0, k_col), blk(0, v_col)], wts["sinks"][0], tq=WINDOW, band_mask=True,
                          grid=(bsz, nblk))

    k_new = proj[:, k_col * KV_DIM:(k_col + 1) * KV_DIM].reshape(bsz, seq_len, N_KV, HEAD_DIM)
    v_new = proj[:, v_col * KV_DIM:(v_col + 1) * KV_DIM].reshape(bsz, seq_len, N_KV, HEAD_DIM)
    if sample:
        k_state = jnp.concatenate([swa_k_cache[0], k_new], axis=1)[:, seq_len:]
        v_state = jnp.concatenate([swa_v_cache[0], v_new], axis=1)[:, seq_len:]
    else:
        k_state, v_state = k_new[:, seq_len - WINDOW:], v_new[:, seq_len - WINDOW:]

    x = _matmul([a, attn], wts["w_out_ab"], 0, **MM_F32W, epilogue="residual", residual=x,
                name="out_ab")

    ffn_states = []

    def conv_ffn(x, layer, next_gain):
        h = _rmsnorm(x, wts["norm_ffn"][layer], BF16)
        prev, prev_layer = (ffn_cache, layer) if sample else (jnp.zeros((1, bsz, 2, 2 * D_FF), F32), 0)
        act, st_g, st_u = _ffn_up(h, wts["w_up_ffn"], wts["conv_w_ffn"], wts["conv_b_ffn"], layer,
                                  prev, prev_layer, seq_len)
        ffn_states.append(jnp.concatenate([st_g, st_u], axis=-1))
        return _matmul([act], wts["w_down_ffn"], layer, **MM_DOWN, epilogue="residual",
                       residual=x, next_gain=next_gain, name="ffn_down")

    x, xg, ss = conv_ffn(x, 0, wts["norm_mix"][1])

    z = _matmul([xg], wts["w_in_c"], 0, **MM_F32W, epilogue="gelu", row_ss=ss, name="in_c")
    cl = min(seq_len, GMLP_CHUNK)
    gated, *maybe_vn = _gmlp_gate(z, wts["ln_g_c"][0], wts["ln_b_c"][0], wts["w_s_c"][0], wts["b_s_c"][0].T,
                                  cl, want_vn=sample)
    gv_state = maybe_vn[0].reshape(1, bsz, seq_len, D_MODEL) if sample else None
    x = _matmul([gated], wts["w_out_c"], 0, **MM_F32W, epilogue="residual", residual=x,
                name="out_c")
    x = conv_ffn(x, 1, None)

    y = _rmsnorm(x, wts["norm_final"], F32)
    return (y.reshape(bsz, seq_len, D_MODEL), conv_state[None], k_state[None], v_state[None],
            jnp.stack(ffn_states), gv_state)


def kernel(x_prompt, x_sample, state_conv_a, cache_swa_k, cache_swa_v, state_ffn_conv, norm_mix, norm_ffn,
           norm_final, w_in_ab, conv_w_a, sinks, w_out_ab, w_in_c, ln_g_c, ln_b_c, w_s_c, b_s_c, w_out_c,
           w_up_ffn, conv_w_ffn, conv_b_ffn, w_down_ffn):
    wts = dict(norm_mix=norm_mix, norm_ffn=norm_ffn, norm_final=norm_final,
               w_in_ab=w_in_ab, conv_w_a=conv_w_a, sinks=sinks, w_out_ab=w_out_ab,
               w_in_c=w_in_c, ln_g_c=ln_g_c, ln_b_c=ln_b_c, w_s_c=w_s_c, b_s_c=b_s_c,
               w_out_c=w_out_c, w_up_ffn=w_up_ffn, conv_w_ffn=conv_w_ffn,
               conv_b_ffn=conv_b_ffn, w_down_ffn=w_down_ffn.astype(BF16))
    y_p, conv_p, k_p, v_p, ffn_p, _ = _trunk(x_prompt, None, None, None, None, wts)
    y_s, conv_s, k_s, v_s, ffn_s, gv_s = _trunk(x_sample, state_conv_a, cache_swa_k, cache_swa_v,
                                                state_ffn_conv, wts)
    return (y_p, y_s, conv_p, conv_s, k_p, k_s, v_p, v_s, ffn_p, ffn_s, gv_s)
```

```python
import functools

import numpy as np
import jax
import jax.numpy as jnp
from jax import lax
from jax.experimental import pallas as pl
from jax.experimental.pallas import tpu as pltpu

F32 = jnp.float32
BF16 = jnp.bfloat16

D_MODEL = 4096
CHUNK = 64
CONV_DIM = 2048
N_HEADS = 32
N_KV = 8
HEAD_DIM = 64
Q_DIM = N_HEADS * HEAD_DIM
KV_DIM = N_KV * HEAD_DIM
WINDOW = 128
GMLP_CHUNK = 128
GMLP_GROUPS = 16
GMLP_GDIM = D_MODEL // GMLP_GROUPS
D_FF = 11008
RMS_EPS = 1e-6
LN_EPS = 1e-5
NEG_INF = -1e30

V7X_VMEM_BYTES = 64 * 1024 * 1024
VMEM_LIMIT = V7X_VMEM_BYTES - 8 * 1024 * 1024

ROWS = 2048
SUB_ROWS = 512
EP_ROWS = 32
FFN_SUB_TILES = (512, 512, 512, 320, 192)
HALO = 8
FF_COLS = 256
N_FF_TILES = D_FF // FF_COLS
LANES = 128
SUBLANES = 8
BF16_SUBLANES = 16
N_SLAB = FF_COLS // LANES
SCR_PITCH = 2
MM_F32W = dict(tm=2048, tn=512, lhs_buffers=1)
MM_DOWN = dict(tm=512, tn=512, lhs_buffers=2)


def _params(*sem):
    return pltpu.CompilerParams(dimension_semantics=sem, vmem_limit_bytes=VMEM_LIMIT)


def _dot(a, b):
    return jnp.dot(a, b, preferred_element_type=F32)


def _as_bf16(x):
    return x if x.dtype == BF16 else x.astype(BF16)


def _rmsnorm_kernel(x_ref, g_ref, o_ref):
    x = x_ref[...]
    r = lax.rsqrt(jnp.mean(x * x, axis=-1, keepdims=True) + RMS_EPS)
    o_ref[...] = (x * r * g_ref[...]).astype(o_ref.dtype)


def _rmsnorm(x, g, out_dtype, tm=512):
    m, d = x.shape
    return pl.pallas_call(
        _rmsnorm_kernel,
        grid=(m // tm,),
        in_specs=[pl.BlockSpec((tm, d), lambda i: (i, 0)),
                  pl.BlockSpec((1, d), lambda i: (0, 0))],
        out_specs=pl.BlockSpec((tm, d), lambda i: (i, 0)),
        out_shape=jax.ShapeDtypeStruct((m, d), out_dtype),
        compiler_params=_params("parallel"),
        name="rmsnorm",
    )(x, g.reshape(1, d))


def _gelu(x):
    return 0.5 * x * (1.0 + lax.erf(x * np.float32(np.sqrt(0.5))))


def _row_scale(ss):
    ms = jnp.sum(ss, axis=-1, keepdims=True) * (1.0 / D_MODEL)
    return jnp.broadcast_to(lax.rsqrt(ms + RMS_EPS), ss.shape)


def _per_lane_tile(fn, x):
    return jnp.concatenate([fn(x[:, k * LANES:(k + 1) * LANES]) for k in range(x.shape[1] // LANES)], axis=1)


def _lane_partial_sumsq(x):
    sq = x * x
    part = sq[:, 0:LANES]
    for k in range(1, sq.shape[1] // LANES):
        part = part + sq[:, k * LANES:(k + 1) * LANES]
    return part


def _mm_kernel(*refs, n_lhs, epilogue, tm, scale_rows, emit_norm):
    refs = list(refs)
    lhs, w_refs = refs[:n_lhs], refs[n_lhs:2 * n_lhs]
    rest = refs[2 * n_lhs:]
    ss_in = rest.pop(0) if scale_rows else None
    res_ref = rest.pop(0) if epilogue == "residual" else None
    gain_ref = rest.pop(0) if emit_norm else None
    o_ref = rest.pop(0)
    xg_ref, ss_out = (rest.pop(0), rest.pop(0)) if emit_norm else (None, None)
    r_scr = rest.pop(0) if scale_rows else None
    first_col = pl.program_id(1) == 0

    if scale_rows:
        @pl.when(first_col)
        def _():
            r_scr[...] = _row_scale(ss_in[...])
    if emit_norm:
        @pl.when(first_col)
        def _():
            ss_out[...] = jnp.zeros(ss_out.shape, F32)

    ws = [_as_bf16(w[...]) for w in w_refs]
    for r in range(tm // SUB_ROWS):
        rows = slice(r * SUB_ROWS, (r + 1) * SUB_ROWS)
        acc = _dot(lhs[0][rows, :], ws[0])
        for l in range(1, n_lhs):
            acc = acc + _dot(lhs[l][rows, :], ws[l])
        if scale_rows:
            scale = r_scr[rows, :]
            acc = _per_lane_tile(lambda t: t * scale, acc)
        if epilogue == "residual":
            acc = res_ref[rows, :] + acc
        elif epilogue == "gelu":
            acc = _gelu(acc)
        o_ref[rows, :] = acc
        if emit_norm:
            xg_ref[rows, :] = (acc * gain_ref[...]).astype(BF16)
            ss_out[rows, :] += _lane_partial_sumsq(acc)


def _matmul(lhs_list, w, layer, *, tm, tn, lhs_buffers, epilogue="none", residual=None, row_ss=None,
            next_gain=None, name):
    m = lhs_list[0].shape[0]
    n = w.shape[2]
    in_specs, args = [], []
    for l in lhs_list:
        in_specs.append(pl.BlockSpec((tm, l.shape[1]), lambda i, j: (i, 0),
                                     pipeline_mode=pl.Buffered(lhs_buffers)))
        args.append(l)
    row_off = 0
    for l in lhs_list:
        k = l.shape[1]
        in_specs.append(pl.BlockSpec((None, k, tn),
                                     functools.partial(lambda i, j, rb: (layer, rb, j), rb=row_off // k)))
        args.append(w)
        row_off += k
    if row_ss is not None:
        in_specs.append(pl.BlockSpec((tm, LANES), lambda i, j: (i, 0)))
        args.append(row_ss)
    if residual is not None:
        in_specs.append(pl.BlockSpec((tm, tn), lambda i, j: (i, j)))
        args.append(residual)
    out_specs = [pl.BlockSpec((tm, tn), lambda i, j: (i, j))]
    out_shape = [jax.ShapeDtypeStruct((m, n), F32)]
    if next_gain is not None:
        in_specs.append(pl.BlockSpec((1, tn), lambda i, j: (0, j)))
        args.append(next_gain.reshape(1, n))
        out_specs += [pl.BlockSpec((tm, tn), lambda i, j: (i, j)), pl.BlockSpec((tm, LANES), lambda i, j: (i, 0))]
        out_shape += [jax.ShapeDtypeStruct((m, n), BF16), jax.ShapeDtypeStruct((m, LANES), F32)]
    outs = pl.pallas_call(
        functools.partial(_mm_kernel, n_lhs=len(lhs_list), epilogue=epilogue, tm=tm,
                          scale_rows=row_ss is not None, emit_norm=next_gain is not None),
        grid=(m // tm, n // tn),
        in_specs=in_specs,
        out_specs=out_specs,
        out_shape=out_shape,
        scratch_shapes=[pltpu.VMEM((tm, LANES), F32)] if row_ss is not None else [],
        compiler_params=_params("parallel", "arbitrary"),
        name=name,
    )(*args)
    return outs if next_gain is not None else outs[0]


def _causal_conv3(x, prev, w, seq_len):
    rows, c = x.shape
    nseq = rows // seq_len
    t = lax.broadcasted_iota(jnp.int32, (rows, 1), 0) & (seq_len - 1)
    p0 = jnp.broadcast_to(prev[:, 0:1, :], (nseq, seq_len, c)).reshape(rows, c)
    p1 = jnp.broadcast_to(prev[:, 1:2, :], (nseq, seq_len, c)).reshape(rows, c)
    x1 = jnp.where(t == 0, p1, pltpu.roll(x, 1, 0))
    x2 = jnp.where(t == 0, p0, jnp.where(t == 1, p1, pltpu.roll(x, 2, 0)))
    return x2 * w[0:1, :] + x1 * w[1:2, :] + x * w[2:3, :]


def _last_two_rows(x, seq_len):
    rows, c = x.shape
    return x.reshape(rows // seq_len, seq_len, c)[:, seq_len - 2:, :]


def _conv_a_kernel(gb_ref, gc_ref, hv_ref, w_ref, prev_ref, a_ref, st_ref, *, seq_len):
    xin = gc_ref[...] * hv_ref[...]
    conv = _causal_conv3(xin, prev_ref[...], w_ref[...], seq_len)
    a_ref[...] = (gb_ref[...] * conv).astype(a_ref.dtype)
    st_ref[...] = _last_two_rows(xin, seq_len)


def _conv_a(proj, conv_w, prev, seq_len, tc=256):
    m = proj.shape[0]
    nseq_tile = ROWS // seq_len
    nj = CONV_DIM // tc
    return pl.pallas_call(
        functools.partial(_conv_a_kernel, seq_len=seq_len),
        grid=(m // ROWS, nj),
        in_specs=[pl.BlockSpec((ROWS, tc), lambda i, j: (i, j)),
                  pl.BlockSpec((ROWS, tc), lambda i, j: (i, j + nj)),
                  pl.BlockSpec((ROWS, tc), lambda i, j: (i, j + 2 * nj)),
                  pl.BlockSpec((3, tc), lambda i, j: (0, j)),
                  pl.BlockSpec((nseq_tile, 2, tc), lambda i, j: (i, 0, j))],
        out_specs=[pl.BlockSpec((ROWS, tc), lambda i, j: (i, j)),
                   pl.BlockSpec((nseq_tile, 2, tc), lambda i, j: (i, 0, j))],
        out_shape=[jax.ShapeDtypeStruct((m, CONV_DIM), BF16),
                   jax.ShapeDtypeStruct((m // seq_len, 2, CONV_DIM), F32)],
        compiler_params=_params("parallel", "parallel"),
        name="conv_a",
    )(proj, proj, proj, conv_w, prev)


def _alibi_slopes():
    exps = np.arange(1, N_HEADS + 1, dtype=np.float32) * (8.0 / N_HEADS)
    return np.power(2.0, -exps).astype(np.float32)


def _attn_kernel(sink_ref, q_ref, kh_ref, kn_ref, vh_ref, vn_ref, o_ref, o_acc, bias_scr, sc_scr, w_scr,
                 *maybe_cap_scr, tq):
    nk = kh_ref.shape[0] + kn_ref.shape[0]
    group = N_HEADS // N_KV
    rows = group * tq
    slopes = _alibi_slopes()

    def per_head(shape, head_in_group, vals):
        out = jnp.full(shape, vals[group - 1], F32)
        for g in range(group - 2, -1, -1):
            out = jnp.where(head_in_group == g, vals[g], out)
        return out

    @pl.when((pl.program_id(0) == 0) & (pl.program_id(1) == 0))
    def _():
        s = lax.broadcasted_iota(jnp.int32, (nk, rows), 0)
        col = lax.broadcasted_iota(jnp.int32, (nk, rows), 1)
        t = col & (tq - 1)
        neg_dist = -jnp.abs(t + (WINDOW - s)).astype(F32)
        for kh in range(N_KV):
            slope = per_head((nk, rows), col // tq, [float(slopes[kh * group + g]) for g in range(group)])
            bias_scr[kh] = slope * neg_dist
        for cap_scr in maybe_cap_scr:
            q_chunk, k_chunk = t // CHUNK, s // CHUNK
            in_band = (k_chunk >= q_chunk) & (k_chunk <= q_chunk + WINDOW // CHUNK)
            for first_block in range(2):
                visible = in_band & (k_chunk >= first_block * (WINDOW // CHUNK))
                cap_scr[first_block] = jnp.where(visible, jnp.inf, NEG_INF)

    first_block = jnp.where(pl.program_id(1) == 0, 1, 0)
    head_of_lane = lax.broadcasted_iota(jnp.int32, (1, rows), 1) // tq

    for kh in range(N_KV):
        cols = slice(kh * HEAD_DIM, (kh + 1) * HEAD_DIM)
        k_all = jnp.concatenate([kh_ref[:, cols], kn_ref[:, cols]], axis=0).astype(BF16)
        q = jnp.concatenate([q_ref[:, h * HEAD_DIM:(h + 1) * HEAD_DIM]
                             for h in range(kh * group, (kh + 1) * group)], axis=0)
        q = (q * (HEAD_DIM ** -0.5)).astype(BF16)
        sc_scr[kh] = lax.dot_general(k_all, q, (((1,), (1,)), ((), ())), preferred_element_type=F32)
    for kh in range(N_KV):
        sink_row = per_head((1, rows), head_of_lane, [sink_ref[kh * group + g] for g in range(group)])
        for c0 in range(0, rows, LANES):
            lanes = slice(c0, c0 + LANES)
            s = sc_scr[kh, :, lanes] + bias_scr[kh, :, lanes]
            for cap_scr in maybe_cap_scr:
                s = jnp.minimum(s, cap_scr[first_block, :, lanes])
            sink = sink_row[:, lanes]
            m = jnp.maximum(jnp.max(s, axis=0, keepdims=True), sink)
            p = jnp.exp(s - m)
            denom = jnp.sum(p, axis=0, keepdims=True) + jnp.exp(sink - m)
            w_scr[kh, :, lanes] = (p * (1.0 / denom)).astype(BF16)
    for kh in range(N_KV):
        cols = slice(kh * HEAD_DIM, (kh + 1) * HEAD_DIM)
        v_all = jnp.concatenate([vh_ref[:, cols], vn_ref[:, cols]], axis=0).astype(BF16)
        out = lax.dot_general(w_scr[kh], v_all, (((0,), (0,)), ((), ())), preferred_element_type=F32)
        for g in range(group):
            h = kh * group + g
            o_acc[:, h * HEAD_DIM:(h + 1) * HEAD_DIM] = out[g * tq:(g + 1) * tq, :]
    o_ref[...] = o_acc[...].astype(o_ref.dtype)


def _attention(q_src, q_map, k_hist, v_hist, hist_maps, k_new, v_new, new_maps, sinks, *, tq, band_mask, grid):
    m = q_src.shape[0]
    nblk = grid[1]
    nk, rows = WINDOW + tq, N_HEADS // N_KV * tq
    return pl.pallas_call(
        functools.partial(_attn_kernel, tq=tq),
        grid=grid,
        in_specs=[pl.BlockSpec(memory_space=pltpu.SMEM),
                  pl.BlockSpec((tq, Q_DIM), q_map),
                  pl.BlockSpec((WINDOW, KV_DIM), hist_maps[0]), pl.BlockSpec((tq, KV_DIM), new_maps[0]),
                  pl.BlockSpec((WINDOW, KV_DIM), hist_maps[1]), pl.BlockSpec((tq, KV_DIM), new_maps[1])],
        out_specs=pl.BlockSpec((tq, Q_DIM), lambda b, c: (b * nblk + c, 0)),
        out_shape=jax.ShapeDtypeStruct((m, Q_DIM), BF16),
        scratch_shapes=[pltpu.VMEM((tq, Q_DIM), F32)]
                       + [pltpu.VMEM((N_KV, nk, rows), F32)] * 2
                       + [pltpu.VMEM((N_KV, nk, rows), BF16)]
                       + ([pltpu.VMEM((2, nk, rows), F32)] if band_mask else []),
        compiler_params=_params("arbitrary", "arbitrary"),
        name="swa",
    )(sinks, q_src, k_hist, k_new, v_hist, v_new)


def _gmlp_kernel(u_ref, v_ref, lng_ref, lnb_ref, ws_ref, bs_ref, o_ref, *maybe_vn_ref, cl):
    v = v_ref[...]
    mu = jnp.mean(v, axis=-1, keepdims=True)
    var = jnp.mean(jnp.square(v - mu), axis=-1, keepdims=True)
    vn = (v - mu) * lax.rsqrt(var + LN_EPS) * lng_ref[...] + lnb_ref[...]
    for vn_ref in maybe_vn_ref:
        vn_ref[...] = vn
    row = lax.broadcasted_iota(jnp.int32, (cl, cl), 0)
    col = lax.broadcasted_iota(jnp.int32, (cl, cl), 1)
    lower = row >= col
    for g in range(GMLP_GROUPS):
        cols = slice(g * GMLP_GDIM, (g + 1) * GMLP_GDIM)
        ws = jnp.where(lower, ws_ref[g, 0:cl, 0:cl], 0.0).astype(BF16)
        s = _dot(ws, vn[:, cols].astype(BF16)) + bs_ref[0:cl, g:g + 1]
        o_ref[:, cols] = (u_ref[:, cols] * s).astype(o_ref.dtype)


def _gmlp_gate(z, ln_g, ln_b, w_s, b_s_t, cl, want_vn):
    m = z.shape[0]
    n_out = 2 if want_vn else 1
    return pl.pallas_call(
        functools.partial(_gmlp_kernel, cl=cl),
        grid=(m // cl,),
        in_specs=[pl.BlockSpec((cl, D_MODEL), lambda i: (i, 0)),
                  pl.BlockSpec((cl, D_MODEL), lambda i: (i, 1)),
                  pl.BlockSpec((1, D_MODEL), lambda i: (0, 0)),
                  pl.BlockSpec((1, D_MODEL), lambda i: (0, 0)),
                  pl.BlockSpec((GMLP_GROUPS, GMLP_CHUNK, GMLP_CHUNK), lambda i: (0, 0, 0)),
                  pl.BlockSpec((GMLP_CHUNK, GMLP_GROUPS), lambda i: (0, 0))],
        out_specs=[pl.BlockSpec((cl, D_MODEL), lambda i: (i, 0)),
                   pl.BlockSpec((cl, D_MODEL), lambda i: (i, 0))][:n_out],
        out_shape=[jax.ShapeDtypeStruct((m, D_MODEL), BF16),
                   jax.ShapeDtypeStruct((m, D_MODEL), F32)][:n_out],
        compiler_params=_params("parallel"),
        name="gmlp_gate",
    )(z, z, ln_g.reshape(1, -1), ln_b.reshape(1, -1), w_s, b_s_t)


def _ffn_up_kernel(h_ref, wg_ref, wu_ref, cwg_ref, cwu_ref, bg_ref, bu_ref, pg_ref, pu_ref, *rest,
                   seq_len, cast_w_down):
    rest = list(rest)
    wd_ref = rest.pop(0) if cast_w_down else None
    act_ref, sg_ref, su_ref = rest.pop(0), rest.pop(0), rest.pop(0)
    wd_bf16_ref = rest.pop(0) if cast_w_down else None
    ug_scr, uu_scr = rest
    if cast_w_down:
        wd_bf16_ref[...] = wd_ref[...].astype(BF16)

    wg, wu = _as_bf16(wg_ref[...]), _as_bf16(wu_ref[...])
    ug_scr[:, 0:SCR_PITCH * HALO, :] = jnp.zeros((N_SLAB, SCR_PITCH * HALO, LANES), F32)
    uu_scr[:, 0:SCR_PITCH * HALO, :] = jnp.zeros((N_SLAB, SCR_PITCH * HALO, LANES), F32)
    row = lax.broadcasted_iota(jnp.int32, (EP_ROWS, 1), 0)

    def rows_at(scr, s, t, n):
        return scr[s, pl.ds(SCR_PITCH * (HALO + t), n, stride=SCR_PITCH), :]

    def conv(scr, prev_ref, cw_ref, b_ref, s, start, z):
        lanes = slice(s * LANES, (s + 1) * LANES)
        x0, x1, x2 = (rows_at(scr, s, start - back, EP_ROWS) for back in range(3))
        if start % seq_len == 0:
            prev = prev_ref[start // seq_len]
            x1 = jnp.where(row == 0, prev[1:2, lanes], x1)
            x2 = jnp.where(row == 0, prev[0:1, lanes], jnp.where(row == 1, prev[1:2, lanes], x2))

        def tap(x, w):
            return (x.reshape(EP_ROWS // SUBLANES, SUBLANES, LANES) * (w + z)[None]).reshape(EP_ROWS, LANES)

        return (tap(x2, cw_ref[0:1, lanes]) + tap(x1, cw_ref[1:2, lanes]) + tap(x0, cw_ref[2:3, lanes])
                + b_ref[:, lanes])

    zs = [jnp.zeros((SUBLANES, LANES), F32)] * N_SLAB

    row0 = 0
    for sub_rows in FFN_SUB_TILES:
        h = h_ref[row0:row0 + sub_rows, :]
        up_g, up_u = _dot(h, wg), _dot(h, wu)
        for s in range(N_SLAB):
            dst = pl.ds(SCR_PITCH * (HALO + row0), sub_rows, stride=SCR_PITCH)
            ug_scr[s, dst, :] = up_g[:, s * LANES:(s + 1) * LANES]
            uu_scr[s, dst, :] = up_u[:, s * LANES:(s + 1) * LANES]
        chunk_starts = range(row0, row0 + sub_rows, EP_ROWS)
        row0 += sub_rows
        for start in chunk_starts:
            end = start + EP_ROWS
            for s in range(N_SLAB):
                lanes = slice(s * LANES, (s + 1) * LANES)
                cg = conv(ug_scr, pg_ref, cwg_ref, bg_ref, s, start, zs[s])
                cu = conv(uu_scr, pu_ref, cwu_ref, bu_ref, s, start, zs[s])
                act = cg * jax.nn.sigmoid(cg) * cu
                act_ref[start:end, lanes] = act.astype(act_ref.dtype)
                bits = lax.bitcast_convert_type(act[0:SUBLANES, :], jnp.uint32)
                zs[s] = ((bits >> 16) >> 16).astype(F32)
                if end % seq_len == 0:
                    sg_ref[end // seq_len - 1, :, lanes] = rows_at(ug_scr, s, end - 2, 2)
                    su_ref[end // seq_len - 1, :, lanes] = rows_at(uu_scr, s, end - 2, 2)


def _ffn_up(h, w_up, conv_w, conv_b, layer, prev, prev_layer, seq_len, w_down=None):
    m = h.shape[0]
    assert sum(FFN_SUB_TILES) == ROWS and all(t % EP_ROWS == 0 for t in FFN_SUB_TILES)
    assert min(seq_len, ROWS) % EP_ROWS == 0 and ROWS % seq_len == 0
    nseq_tile = ROWS // seq_len
    nj = N_FF_TILES
    st_spec_g = pl.BlockSpec((None, nseq_tile, 2, FF_COLS), lambda i, j: (prev_layer, i, 0, j))
    st_spec_u = pl.BlockSpec((None, nseq_tile, 2, FF_COLS), lambda i, j: (prev_layer, i, 0, j + nj))
    conv_b = conv_b.reshape(conv_b.shape[0], 1, -1)
    in_specs = [pl.BlockSpec((ROWS, D_MODEL), lambda i, j: (i, 0), pipeline_mode=pl.Buffered(1)),
                pl.BlockSpec((None, D_MODEL, FF_COLS), lambda i, j: (layer, 0, j)),
                pl.BlockSpec((None, D_MODEL, FF_COLS), lambda i, j: (layer, 0, j + nj)),
                pl.BlockSpec((None, 3, FF_COLS), lambda i, j: (layer, 0, j)),
                pl.BlockSpec((None, 3, FF_COLS), lambda i, j: (layer, 0, j + nj)),
                pl.BlockSpec((None, 1, FF_COLS), lambda i, j: (layer, 0, j)),
                pl.BlockSpec((None, 1, FF_COLS), lambda i, j: (layer, 0, j + nj)),
                st_spec_g, st_spec_u]
    args = [h, w_up, w_up, conv_w, conv_w, conv_b, conv_b, prev, prev]
    out_specs = [pl.BlockSpec((ROWS, FF_COLS), lambda i, j: (i, j)),
                 pl.BlockSpec((nseq_tile, 2, FF_COLS), lambda i, j: (i, 0, j)),
                 pl.BlockSpec((nseq_tile, 2, FF_COLS), lambda i, j: (i, 0, j))]
    out_shape = [jax.ShapeDtypeStruct((m, D_FF), BF16),
                 jax.ShapeDtypeStruct((m // seq_len, 2, D_FF), F32),
                 jax.ShapeDtypeStruct((m // seq_len, 2, D_FF), F32)]
    if w_down is not None:
        n_steps = (m // ROWS) * nj
        cast_rows = D_FF // n_steps
        assert cast_rows * n_steps == D_FF and cast_rows % BF16_SUBLANES == 0
        in_specs.append(pl.BlockSpec((None, cast_rows, D_MODEL), lambda i, j: (layer, i * nj + j, 0)))
        args.append(w_down)
        out_specs.append(pl.BlockSpec((cast_rows, D_MODEL), lambda i, j: (i * nj + j, 0)))
        out_shape.append(jax.ShapeDtypeStruct((D_FF, D_MODEL), BF16))
    return pl.pallas_call(
        functools.partial(_ffn_up_kernel, seq_len=seq_len, cast_w_down=w_down is not None),
        grid=(m // ROWS, nj),
        in_specs=in_specs,
        out_specs=out_specs,
        out_shape=out_shape,
        scratch_shapes=[pltpu.VMEM((N_SLAB, SCR_PITCH * (HALO + ROWS), LANES), F32)] * 2,
        compiler_params=_params("parallel", "arbitrary"),
        name="ffn_up",
    )(*args)


def _trunk(x3, conv_a_cache, swa_k_cache, swa_v_cache, ffn_cache, wts, w_down_bf16):
    bsz, seq_len, _ = x3.shape
    m = bsz * seq_len
    sample = ffn_cache is not None
    x = x3.reshape(m, D_MODEL)

    h = _rmsnorm(x, wts["norm_mix"][0], BF16)
    proj = _matmul([h], wts["w_in_ab"], 0, **MM_F32W, name="in_ab")
    prev_a = conv_a_cache[0] if sample else jnp.zeros((bsz, 2, CONV_DIM), F32)
    a, conv_state = _conv_a(proj, wts["conv_w_a"][0], prev_a, seq_len)

    q_col = 3 * CONV_DIM // Q_DIM
    k_col = (3 * CONV_DIM + Q_DIM) // KV_DIM
    v_col = k_col + 1
    if sample:
        k_hist = swa_k_cache[0].reshape(bsz * WINDOW, KV_DIM)
        v_hist = swa_v_cache[0].reshape(bsz * WINDOW, KV_DIM)
        hist = [lambda b, c: (b, 0)] * 2
        new = [lambda b, c: (b, k_col), lambda b, c: (b, v_col)]
        attn = _attention(proj, lambda b, c: (b, q_col), k_hist, v_hist, hist, proj, proj, new,
                          wts["sinks"][0], tq=seq_len, band_mask=False, grid=(bsz, 1))
    else:
        nblk = seq_len // WINDOW
        def blk(back, col):
            return lambda b, c: (b * nblk + jnp.maximum(c - back, 0), col)
        attn = _attention(proj, blk(0, q_col), proj, proj, [blk(1, k_col), blk(1, v_col)], proj, proj,
                          [blk(0, k_col), blk(0, v_col)], wts["sinks"][0], tq=WINDOW, band_mask=True,
                          grid=(bsz, nblk))

    k_new = proj[:, k_col * KV_DIM:(k_col + 1) * KV_DIM].reshape(bsz, seq_len, N_KV, HEAD_DIM)
    v_new = proj[:, v_col * KV_DIM:(v_col + 1) * KV_DIM].reshape(bsz, seq_len, N_KV, HEAD_DIM)
    if sample:
        k_state = jnp.concatenate([swa_k_cache[0], k_new], axis=1)[:, seq_len:]
        v_state = jnp.concatenate([swa_v_cache[0], v_new], axis=1)[:, seq_len:]
    else:
        k_state, v_state = k_new[:, seq_len - WINDOW:], v_new[:, seq_len - WINDOW:]

    x = _matmul([a, attn], wts["w_out_ab"], 0, **MM_F32W, epilogue="residual", residual=x,
                name="out_ab")

    ffn_states = []

    def conv_ffn(x, layer, next_gain):
        h = _rmsnorm(x, wts["norm_ffn"][layer], BF16)
        prev, prev_layer = (ffn_cache, layer) if sample else (jnp.zeros((1, bsz, 2, 2 * D_FF), F32), 0)
        act, st_g, st_u, *maybe_wd = _ffn_up(h, wts["w_up_ffn"], wts["conv_w_ffn"], wts["conv_b_ffn"], layer,
                                             prev, prev_layer, seq_len,
                                             w_down=None if sample else wts["w_down_ffn"])
        if not sample:
            w_down_bf16[layer] = maybe_wd[0][None]
        ffn_states.append(jnp.concatenate([st_g, st_u], axis=-1))
        return _matmul([act], w_down_bf16[layer], 0, **MM_DOWN, epilogue="residual",
                       residual=x, next_gain=next_gain, name="ffn_down")

    x, xg, ss = conv_ffn(x, 0, wts["norm_mix"][1])

    z = _matmul([xg], wts["w_in_c"], 0, **MM_F32W, epilogue="gelu", row_ss=ss, name="in_c")
    cl = min(seq_len, GMLP_CHUNK)
    gated, *maybe_vn = _gmlp_gate(z, wts["ln_g_c"][0], wts["ln_b_c"][0], wts["w_s_c"][0], wts["b_s_c"][0].T,
                                  cl, want_vn=sample)
    gv_state = maybe_vn[0].reshape(1, bsz, seq_len, D_MODEL) if sample else None
    x = _matmul([gated], wts["w_out_c"], 0, **MM_F32W, epilogue="residual", residual=x,
                name="out_c")
    x = conv_ffn(x, 1, None)

    y = _rmsnorm(x, wts["norm_final"], F32)
    return (y.reshape(bsz, seq_len, D_MODEL), conv_state[None], k_state[None], v_state[None],
            jnp.stack(ffn_states), gv_state)


def kernel(x_prompt, x_sample, state_conv_a, cache_swa_k, cache_swa_v, state_ffn_conv, norm_mix, norm_ffn,
           norm_final, w_in_ab, conv_w_a, sinks, w_out_ab, w_in_c, ln_g_c, ln_b_c, w_s_c, b_s_c, w_out_c,
           w_up_ffn, conv_w_ffn, conv_b_ffn, w_down_ffn):
    wts = dict(norm_mix=norm_mix, norm_ffn=norm_ffn, norm_final=norm_final,
               w_in_ab=w_in_ab, conv_w_a=conv_w_a, sinks=sinks, w_out_ab=w_out_ab,
               w_in_c=w_in_c, ln_g_c=ln_g_c, ln_b_c=ln_b_c, w_s_c=w_s_c, b_s_c=b_s_c,
               w_out_c=w_out_c, w_up_ffn=w_up_ffn, conv_w_ffn=conv_w_ffn,
               conv_b_ffn=conv_b_ffn, w_down_ffn=w_down_ffn)
    w_down_bf16 = {}
    y_p, conv_p, k_p, v_p, ffn_p, _ = _trunk(x_prompt, None, None, None, None, wts, w_down_bf16)
    y_s, conv_s, k_s, v_s, ffn_s, gv_s = _trunk(x_sample, state_conv_a, cache_swa_k, cache_swa_v,
                                                state_ffn_conv, wts, w_down_bf16)
    return (y_p, y_s, conv_p, conv_s, k_p, k_s, v_p, v_s, ffn_p, ffn_s, gv_s)
```

```python
import functools

import numpy as np
import jax
import jax.numpy as jnp
from jax import lax
from jax.experimental import pallas as pl
from jax.experimental.pallas import tpu as pltpu

F32 = jnp.float32
BF16 = jnp.bfloat16

D_MODEL = 4096
CHUNK = 64
CONV_DIM = 2048
N_HEADS = 32
N_KV = 8
HEAD_DIM = 64
Q_DIM = N_HEADS * HEAD_DIM
KV_DIM = N_KV * HEAD_DIM
WINDOW = 128
GMLP_CHUNK = 128
GMLP_GROUPS = 16
GMLP_GDIM = D_MODEL // GMLP_GROUPS
D_FF = 11008
RMS_EPS = 1e-6
LN_EPS = 1e-5
NEG_INF = -1e30

V7X_VMEM_BYTES = 64 * 1024 * 1024
VMEM_LIMIT = V7X_VMEM_BYTES - 8 * 1024 * 1024

ROWS = 2048
SUB_ROWS = 512
EP_ROWS = 32
FFN_SUB_TILES = (512, 512, 512, 320, 192)
HALO = 8
FF_COLS = 256
N_FF_TILES = D_FF // FF_COLS
LANES = 128
SUBLANES = 8
BF16_SUBLANES = 16
N_SLAB = FF_COLS // LANES
SCR_PITCH = 2
MM_F32W = dict(tm=2048, tn=512, lhs_buffers=1)
MM_DOWN = dict(tm=512, tn=512, lhs_buffers=2)


def _params(*sem):
    return pltpu.CompilerParams(dimension_semantics=sem, vmem_limit_bytes=VMEM_LIMIT)


def _dot(a, b):
    return jnp.dot(a, b, preferred_element_type=F32)


def _as_bf16(x):
    return x if x.dtype == BF16 else x.astype(BF16)


def _rmsnorm_kernel(x_ref, g_ref, o_ref):
    x = x_ref[...]
    r = lax.rsqrt(jnp.mean(x * x, axis=-1, keepdims=True) + RMS_EPS)
    o_ref[...] = (x * r * g_ref[...]).astype(o_ref.dtype)


def _rmsnorm(x, g, out_dtype, tm=512):
    m, d = x.shape
    return pl.pallas_call(
        _rmsnorm_kernel,
        grid=(m // tm,),
        in_specs=[pl.BlockSpec((tm, d), lambda i: (i, 0)),
                  pl.BlockSpec((1, d), lambda i: (0, 0))],
        out_specs=pl.BlockSpec((tm, d), lambda i: (i, 0)),
        out_shape=jax.ShapeDtypeStruct((m, d), out_dtype),
        compiler_params=_params("parallel"),
        name="rmsnorm",
    )(x, g.reshape(1, d))


def _gelu(x):
    return 0.5 * x * (1.0 + lax.erf(x * np.float32(np.sqrt(0.5))))


def _row_scale(ss):
    ms = jnp.sum(ss, axis=-1, keepdims=True) * (1.0 / D_MODEL)
    return jnp.broadcast_to(lax.rsqrt(ms + RMS_EPS), ss.shape)


def _per_lane_tile(fn, x):
    return jnp.concatenate([fn(x[:, k * LANES:(k + 1) * LANES]) for k in range(x.shape[1] // LANES)], axis=1)


def _lane_partial_sumsq(x):
    sq = x * x
    part = sq[:, 0:LANES]
    for k in range(1, sq.shape[1] // LANES):
        part = part + sq[:, k * LANES:(k + 1) * LANES]
    return part


def _mm_kernel(*refs, n_lhs, epilogue, tm, scale_rows, emit_norm):
    refs = list(refs)
    lhs, w_refs = refs[:n_lhs], refs[n_lhs:2 * n_lhs]
    rest = refs[2 * n_lhs:]
    ss_in = rest.pop(0) if scale_rows else None
    res_ref = rest.pop(0) if epilogue == "residual" else None
    gain_ref = rest.pop(0) if emit_norm else None
    o_ref = rest.pop(0)
    xg_ref, ss_out = (rest.pop(0), rest.pop(0)) if emit_norm else (None, None)
    r_scr = rest.pop(0) if scale_rows else None
    first_col = pl.program_id(1) == 0

    if scale_rows:
        @pl.when(first_col)
        def _():
            r_scr[...] = _row_scale(ss_in[...])
    if emit_norm:
        @pl.when(first_col)
        def _():
            ss_out[...] = jnp.zeros(ss_out.shape, F32)

    ws = [_as_bf16(w[...]) for w in w_refs]
    for r in range(tm // SUB_ROWS):
        rows = slice(r * SUB_ROWS, (r + 1) * SUB_ROWS)
        acc = _dot(lhs[0][rows, :], ws[0])
        for l in range(1, n_lhs):
            acc = acc + _dot(lhs[l][rows, :], ws[l])
        if scale_rows:
            scale = r_scr[rows, :]
            acc = _per_lane_tile(lambda t: t * scale, acc)
        if epilogue == "residual":
            acc = res_ref[rows, :] + acc
        elif epilogue == "gelu":
            acc = _gelu(acc)
        o_ref[rows, :] = acc
        if emit_norm:
            xg_ref[rows, :] = (acc * gain_ref[...]).astype(BF16)
            ss_out[rows, :] += _lane_partial_sumsq(acc)


def _matmul(lhs_list, w, layer, *, tm, tn, lhs_buffers, epilogue="none", residual=None, row_ss=None,
            next_gain=None, name):
    m = lhs_list[0].shape[0]
    n = w.shape[2]
    in_specs, args = [], []
    for l in lhs_list:
        in_specs.append(pl.BlockSpec((tm, l.shape[1]), lambda i, j: (i, 0),
                                     pipeline_mode=pl.Buffered(lhs_buffers)))
        args.append(l)
    row_off = 0
    for l in lhs_list:
        k = l.shape[1]
        in_specs.append(pl.BlockSpec((None, k, tn),
                                     functools.partial(lambda i, j, rb: (layer, rb, j), rb=row_off // k)))
        args.append(w)
        row_off += k
    if row_ss is not None:
        in_specs.append(pl.BlockSpec((tm, LANES), lambda i, j: (i, 0)))
        args.append(row_ss)
    if residual is not None:
        in_specs.append(pl.BlockSpec((tm, tn), lambda i, j: (i, j)))
        args.append(residual)
    out_specs = [pl.BlockSpec((tm, tn), lambda i, j: (i, j))]
    out_shape = [jax.ShapeDtypeStruct((m, n), F32)]
    if next_gain is not None:
        in_specs.append(pl.BlockSpec((1, tn), lambda i, j: (0, j)))
        args.append(next_gain.reshape(1, n))
        out_specs += [pl.BlockSpec((tm, tn), lambda i, j: (i, j)), pl.BlockSpec((tm, LANES), lambda i, j: (i, 0))]
        out_shape += [jax.ShapeDtypeStruct((m, n), BF16), jax.ShapeDtypeStruct((m, LANES), F32)]
    outs = pl.pallas_call(
        functools.partial(_mm_kernel, n_lhs=len(lhs_list), epilogue=epilogue, tm=tm,
                          scale_rows=row_ss is not None, emit_norm=next_gain is not None),
        grid=(m // tm, n // tn),
        in_specs=in_specs,
        out_specs=out_specs,
        out_shape=out_shape,
        scratch_shapes=[pltpu.VMEM((tm, LANES), F32)] if row_ss is not None else [],
        compiler_params=_params("parallel", "arbitrary"),
        name=name,
    )(*args)
    return outs if next_gain is not None else outs[0]


def _causal_conv3(x, prev, w, seq_len):
    rows, c = x.shape
    nseq = rows // seq_len
    t = lax.broadcasted_iota(jnp.int32, (rows, 1), 0) & (seq_len - 1)
    p0 = jnp.broadcast_to(prev[:, 0:1, :], (nseq, seq_len, c)).reshape(rows, c)
    p1 = jnp.broadcast_to(prev[:, 1:2, :], (nseq, seq_len, c)).reshape(rows, c)
    x1 = jnp.where(t == 0, p1, pltpu.roll(x, 1, 0))
    x2 = jnp.where(t == 0, p0, jnp.where(t == 1, p1, pltpu.roll(x, 2, 0)))
    return x2 * w[0:1, :] + x1 * w[1:2, :] + x * w[2:3, :]


def _alibi_slopes():
    exps = np.arange(1, N_HEADS + 1, dtype=np.float32) * (8.0 / N_HEADS)
    return np.power(2.0, -exps).astype(np.float32)


def _attn_kernel(sink_ref, q_ref, kh_ref, kn_ref, vh_ref, vn_ref, gb_ref, gc_ref, hv_ref, cw_ref, prev_ref,
                 o_ref, a_ref, cs_ref, o_acc, bias_scr, sc_scr, w_scr, carry_scr, *maybe_cap_scr, tq):
    @pl.when(pl.program_id(1) == 0)
    def _():
        carry_scr[...] = prev_ref[0]

    xin = gc_ref[...] * hv_ref[...]
    conv = _causal_conv3(xin, carry_scr[...][None], cw_ref[...], tq)
    a_ref[...] = (gb_ref[...] * conv).astype(a_ref.dtype)
    carry_scr[...] = xin[tq - 2:tq, :]
    cs_ref[0] = xin[tq - 2:tq, :]

    nk = kh_ref.shape[0] + kn_ref.shape[0]
    group = N_HEADS // N_KV
    rows = group * tq
    slopes = _alibi_slopes()

    def per_head(shape, head_in_group, vals):
        out = jnp.full(shape, vals[group - 1], F32)
        for g in range(group - 2, -1, -1):
            out = jnp.where(head_in_group == g, vals[g], out)
        return out

    @pl.when((pl.program_id(0) == 0) & (pl.program_id(1) == 0))
    def _():
        s = lax.broadcasted_iota(jnp.int32, (nk, rows), 0)
        col = lax.broadcasted_iota(jnp.int32, (nk, rows), 1)
        t = col & (tq - 1)
        neg_dist = -jnp.abs(t + (WINDOW - s)).astype(F32)
        for kh in range(N_KV):
            slope = per_head((nk, rows), col // tq, [float(slopes[kh * group + g]) for g in range(group)])
            bias_scr[kh] = slope * neg_dist
        for cap_scr in maybe_cap_scr:
            q_chunk, k_chunk = t // CHUNK, s // CHUNK
            in_band = (k_chunk >= q_chunk) & (k_chunk <= q_chunk + WINDOW // CHUNK)
            for first_block in range(2):
                visible = in_band & (k_chunk >= first_block * (WINDOW // CHUNK))
                cap_scr[first_block] = jnp.where(visible, jnp.inf, NEG_INF)

    first_block = jnp.where(pl.program_id(1) == 0, 1, 0)
    head_of_lane = lax.broadcasted_iota(jnp.int32, (1, rows), 1) // tq

    for kh in range(N_KV):
        cols = slice(kh * HEAD_DIM, (kh + 1) * HEAD_DIM)
        k_all = jnp.concatenate([kh_ref[:, cols], kn_ref[:, cols]], axis=0).astype(BF16)
        q = jnp.concatenate([q_ref[:, h * HEAD_DIM:(h + 1) * HEAD_DIM]
                             for h in range(kh * group, (kh + 1) * group)], axis=0)
        q = (q * (HEAD_DIM ** -0.5)).astype(BF16)
        sc_scr[kh] = lax.dot_general(k_all, q, (((1,), (1,)), ((), ())), preferred_element_type=F32)
    for kh in range(N_KV):
        sink_row = per_head((1, rows), head_of_lane, [sink_ref[kh * group + g] for g in range(group)])
        for c0 in range(0, rows, LANES):
            lanes = slice(c0, c0 + LANES)
            s = sc_scr[kh, :, lanes] + bias_scr[kh, :, lanes]
            for cap_scr in maybe_cap_scr:
                s = jnp.minimum(s, cap_scr[first_block, :, lanes])
            sink = sink_row[:, lanes]
            m = jnp.maximum(jnp.max(s, axis=0, keepdims=True), sink)
            p = jnp.exp(s - m)
            denom = jnp.sum(p, axis=0, keepdims=True) + jnp.exp(sink - m)
            w_scr[kh, :, lanes] = (p * (1.0 / denom)).astype(BF16)
    for kh in range(N_KV):
        cols = slice(kh * HEAD_DIM, (kh + 1) * HEAD_DIM)
        v_all = jnp.concatenate([vh_ref[:, cols], vn_ref[:, cols]], axis=0).astype(BF16)
        out = lax.dot_general(w_scr[kh], v_all, (((0,), (0,)), ((), ())), preferred_element_type=F32)
        for g in range(group):
            h = kh * group + g
            o_acc[:, h * HEAD_DIM:(h + 1) * HEAD_DIM] = out[g * tq:(g + 1) * tq, :]
    o_ref[...] = o_acc[...].astype(o_ref.dtype)


def _mixers_ab(proj, row_map, q_col, k_hist, v_hist, hist_maps, new_maps, sinks, conv_w, conv_prev,
               *, tq, band_mask, grid):
    m = proj.shape[0]
    nk, rows = WINDOW + tq, N_HEADS // N_KV * tq

    def proj_cols(col):
        return lambda b, c: (row_map(b, c), col)

    conv_specs = [pl.BlockSpec((tq, CONV_DIM), proj_cols(col)) for col in range(3)]
    return pl.pallas_call(
        functools.partial(_attn_kernel, tq=tq),
        grid=grid,
        in_specs=[pl.BlockSpec(memory_space=pltpu.SMEM),
                  pl.BlockSpec((tq, Q_DIM), proj_cols(q_col)),
                  pl.BlockSpec((WINDOW, KV_DIM), hist_maps[0]), pl.BlockSpec((tq, KV_DIM), new_maps[0]),
                  pl.BlockSpec((WINDOW, KV_DIM), hist_maps[1]), pl.BlockSpec((tq, KV_DIM), new_maps[1])]
                 + conv_specs
                 + [pl.BlockSpec((3, CONV_DIM), lambda b, c: (0, 0)),
                    pl.BlockSpec((1, 2, CONV_DIM), lambda b, c: (b, 0, 0))],
        out_specs=[pl.BlockSpec((tq, Q_DIM), proj_cols(0)),
                   pl.BlockSpec((tq, CONV_DIM), proj_cols(0)),
                   pl.BlockSpec((1, 2, CONV_DIM), lambda b, c: (b, 0, 0))],
        out_shape=[jax.ShapeDtypeStruct((m, Q_DIM), BF16),
                   jax.ShapeDtypeStruct((m, CONV_DIM), BF16),
                   jax.ShapeDtypeStruct((grid[0], 2, CONV_DIM), F32)],
        scratch_shapes=[pltpu.VMEM((tq, Q_DIM), F32)]
                       + [pltpu.VMEM((N_KV, nk, rows), F32)] * 2
                       + [pltpu.VMEM((N_KV, nk, rows), BF16)]
                       + [pltpu.VMEM((2, CONV_DIM), F32)]
                       + ([pltpu.VMEM((2, nk, rows), F32)] if band_mask else []),
        compiler_params=_params("arbitrary", "arbitrary"),
        name="swa",
    )(sinks, proj, k_hist, proj, v_hist, proj, proj, proj, proj, conv_w, conv_prev)


def _gmlp_kernel(u_ref, v_ref, lng_ref, lnb_ref, ws_ref, bs_ref, o_ref, *maybe_vn_ref, cl):
    v = v_ref[...]
    mu = jnp.mean(v, axis=-1, keepdims=True)
    var = jnp.mean(jnp.square(v - mu), axis=-1, keepdims=True)
    vn = (v - mu) * lax.rsqrt(var + LN_EPS) * lng_ref[...] + lnb_ref[...]
    for vn_ref in maybe_vn_ref:
        vn_ref[...] = vn
    row = lax.broadcasted_iota(jnp.int32, (cl, cl), 0)
    col = lax.broadcasted_iota(jnp.int32, (cl, cl), 1)
    lower = row >= col
    for g in range(GMLP_GROUPS):
        cols = slice(g * GMLP_GDIM, (g + 1) * GMLP_GDIM)
        ws = jnp.where(lower, ws_ref[g, 0:cl, 0:cl], 0.0).astype(BF16)
        s = _dot(ws, vn[:, cols].astype(BF16)) + bs_ref[0:cl, g:g + 1]
        o_ref[:, cols] = (u_ref[:, cols] * s).astype(o_ref.dtype)


def _gmlp_gate(z, ln_g, ln_b, w_s, b_s_t, cl, want_vn):
    m = z.shape[0]
    n_out = 2 if want_vn else 1
    return pl.pallas_call(
        functools.partial(_gmlp_kernel, cl=cl),
        grid=(m // cl,),
        in_specs=[pl.BlockSpec((cl, D_MODEL), lambda i: (i, 0)),
                  pl.BlockSpec((cl, D_MODEL), lambda i: (i, 1)),
                  pl.BlockSpec((1, D_MODEL), lambda i: (0, 0)),
                  pl.BlockSpec((1, D_MODEL), lambda i: (0, 0)),
                  pl.BlockSpec((GMLP_GROUPS, GMLP_CHUNK, GMLP_CHUNK), lambda i: (0, 0, 0)),
                  pl.BlockSpec((GMLP_CHUNK, GMLP_GROUPS), lambda i: (0, 0))],
        out_specs=[pl.BlockSpec((cl, D_MODEL), lambda i: (i, 0)),
                   pl.BlockSpec((cl, D_MODEL), lambda i: (i, 0))][:n_out],
        out_shape=[jax.ShapeDtypeStruct((m, D_MODEL), BF16),
                   jax.ShapeDtypeStruct((m, D_MODEL), F32)][:n_out],
        compiler_params=_params("parallel"),
        name="gmlp_gate",
    )(z, z, ln_g.reshape(1, -1), ln_b.reshape(1, -1), w_s, b_s_t)


def _ffn_up_kernel(h_ref, wg_ref, wu_ref, cwg_ref, cwu_ref, bg_ref, bu_ref, pg_ref, pu_ref, *rest,
                   seq_len, cast_w_down):
    rest = list(rest)
    wd_ref = rest.pop(0) if cast_w_down else None
    act_ref, sg_ref, su_ref = rest.pop(0), rest.pop(0), rest.pop(0)
    wd_bf16_ref = rest.pop(0) if cast_w_down else None
    ug_scr, uu_scr = rest
    if cast_w_down:
        wd_bf16_ref[...] = wd_ref[...].astype(BF16)

    wg, wu = _as_bf16(wg_ref[...]), _as_bf16(wu_ref[...])
    ug_scr[:, 0:SCR_PITCH * HALO, :] = jnp.zeros((N_SLAB, SCR_PITCH * HALO, LANES), F32)
    uu_scr[:, 0:SCR_PITCH * HALO, :] = jnp.zeros((N_SLAB, SCR_PITCH * HALO, LANES), F32)
    row = lax.broadcasted_iota(jnp.int32, (EP_ROWS, 1), 0)

    def rows_at(scr, s, t, n):
        return scr[s, pl.ds(SCR_PITCH * (HALO + t), n, stride=SCR_PITCH), :]

    def conv(scr, prev_ref, cw_ref, b_ref, s, start, z):
        lanes = slice(s * LANES, (s + 1) * LANES)
        x0, x1, x2 = (rows_at(scr, s, start - back, EP_ROWS) for back in range(3))
        if start % seq_len == 0:
            prev = prev_ref[start // seq_len]
            x1 = jnp.where(row == 0, prev[1:2, lanes], x1)
            x2 = jnp.where(row == 0, prev[0:1, lanes], jnp.where(row == 1, prev[1:2, lanes], x2))

        def tap(x, w):
            return (x.reshape(EP_ROWS // SUBLANES, SUBLANES, LANES) * (w + z)[None]).reshape(EP_ROWS, LANES)

        return (tap(x2, cw_ref[0:1, lanes]) + tap(x1, cw_ref[1:2, lanes]) + tap(x0, cw_ref[2:3, lanes])
                + b_ref[:, lanes])

    zs = [jnp.zeros((SUBLANES, LANES), F32)] * N_SLAB

    row0 = 0
    for sub_rows in FFN_SUB_TILES:
        h = h_ref[row0:row0 + sub_rows, :]
        up_g, up_u = _dot(h, wg), _dot(h, wu)
        for s in range(N_SLAB):
            dst = pl.ds(SCR_PITCH * (HALO + row0), sub_rows, stride=SCR_PITCH)
            ug_scr[s, dst, :] = up_g[:, s * LANES:(s + 1) * LANES]
            uu_scr[s, dst, :] = up_u[:, s * LANES:(s + 1) * LANES]
        chunk_starts = range(row0, row0 + sub_rows, EP_ROWS)
        row0 += sub_rows
        for start in chunk_starts:
            end = start + EP_ROWS
            for s in range(N_SLAB):
                lanes = slice(s * LANES, (s + 1) * LANES)
                cg = conv(ug_scr, pg_ref, cwg_ref, bg_ref, s, start, zs[s])
                cu = conv(uu_scr, pu_ref, cwu_ref, bu_ref, s, start, zs[s])
                act = cg * jax.nn.sigmoid(cg) * cu
                act_ref[start:end, lanes] = act.astype(act_ref.dtype)
                bits = lax.bitcast_convert_type(act[0:SUBLANES, :], jnp.uint32)
                zs[s] = ((bits >> 16) >> 16).astype(F32)
                if end % seq_len == 0:
                    sg_ref[end // seq_len - 1, :, lanes] = rows_at(ug_scr, s, end - 2, 2)
                    su_ref[end // seq_len - 1, :, lanes] = rows_at(uu_scr, s, end - 2, 2)


def _ffn_up(h, w_up, conv_w, conv_b, layer, prev, prev_layer, seq_len, w_down=None):
    m = h.shape[0]
    assert sum(FFN_SUB_TILES) == ROWS and all(t % EP_ROWS == 0 for t in FFN_SUB_TILES)
    assert min(seq_len, ROWS) % EP_ROWS == 0 and ROWS % seq_len == 0
    nseq_tile = ROWS // seq_len
    nj = N_FF_TILES
    st_spec_g = pl.BlockSpec((None, nseq_tile, 2, FF_COLS), lambda i, j: (prev_layer, i, 0, j))
    st_spec_u = pl.BlockSpec((None, nseq_tile, 2, FF_COLS), lambda i, j: (prev_layer, i, 0, j + nj))
    conv_b = conv_b.reshape(conv_b.shape[0], 1, -1)
    in_specs = [pl.BlockSpec((ROWS, D_MODEL), lambda i, j: (i, 0), pipeline_mode=pl.Buffered(1)),
                pl.BlockSpec((None, D_MODEL, FF_COLS), lambda i, j: (layer, 0, j)),
                pl.BlockSpec((None, D_MODEL, FF_COLS), lambda i, j: (layer, 0, j + nj)),
                pl.BlockSpec((None, 3, FF_COLS), lambda i, j: (layer, 0, j)),
                pl.BlockSpec((None, 3, FF_COLS), lambda i, j: (layer, 0, j + nj)),
                pl.BlockSpec((None, 1, FF_COLS), lambda i, j: (layer, 0, j)),
                pl.BlockSpec((None, 1, FF_COLS), lambda i, j: (layer, 0, j + nj)),
                st_spec_g, st_spec_u]
    args = [h, w_up, w_up, conv_w, conv_w, conv_b, conv_b, prev, prev]
    out_specs = [pl.BlockSpec((ROWS, FF_COLS), lambda i, j: (i, j)),
                 pl.BlockSpec((nseq_tile, 2, FF_COLS), lambda i, j: (i, 0, j)),
                 pl.BlockSpec((nseq_tile, 2, FF_COLS), lambda i, j: (i, 0, j))]
    out_shape = [jax.ShapeDtypeStruct((m, D_FF), BF16),
                 jax.ShapeDtypeStruct((m // seq_len, 2, D_FF), F32),
                 jax.ShapeDtypeStruct((m // seq_len, 2, D_FF), F32)]
    if w_down is not None:
        n_steps = (m // ROWS) * nj
        cast_rows = D_FF // n_steps
        assert cast_rows * n_steps == D_FF and cast_rows % BF16_SUBLANES == 0
        in_specs.append(pl.BlockSpec((None, cast_rows, D_MODEL), lambda i, j: (layer, i * nj + j, 0)))
        args.append(w_down)
        out_specs.append(pl.BlockSpec((cast_rows, D_MODEL), lambda i, j: (i * nj + j, 0)))
        out_shape.append(jax.ShapeDtypeStruct((D_FF, D_MODEL), BF16))
    return pl.pallas_call(
        functools.partial(_ffn_up_kernel, seq_len=seq_len, cast_w_down=w_down is not None),
        grid=(m // ROWS, nj),
        in_specs=in_specs,
        out_specs=out_specs,
        out_shape=out_shape,
        scratch_shapes=[pltpu.VMEM((N_SLAB, SCR_PITCH * (HALO + ROWS), LANES), F32)] * 2,
        compiler_params=_params("parallel", "arbitrary"),
        name="ffn_up",
    )(*args)


def _trunk(x3, conv_a_cache, swa_k_cache, swa_v_cache, ffn_cache, wts, w_down_bf16):
    bsz, seq_len, _ = x3.shape
    m = bsz * seq_len
    sample = ffn_cache is not None
    x = x3.reshape(m, D_MODEL)

    h = _rmsnorm(x, wts["norm_mix"][0], BF16)
    proj = _matmul([h], wts["w_in_ab"], 0, **MM_F32W, name="in_ab")
    prev_a = conv_a_cache[0] if sample else jnp.zeros((bsz, 2, CONV_DIM), F32)

    q_col = 3 * CONV_DIM // Q_DIM
    k_col = (3 * CONV_DIM + Q_DIM) // KV_DIM
    v_col = k_col + 1
    if sample:
        k_hist = swa_k_cache[0].reshape(bsz * WINDOW, KV_DIM)
        v_hist = swa_v_cache[0].reshape(bsz * WINDOW, KV_DIM)
        hist = [lambda b, c: (b, 0)] * 2
        new = [lambda b, c: (b, k_col), lambda b, c: (b, v_col)]
        attn, a, conv_state = _mixers_ab(proj, lambda b, c: b, q_col, k_hist, v_hist, hist, new,
                                         wts["sinks"][0], wts["conv_w_a"][0], prev_a,
                                         tq=seq_len, band_mask=False, grid=(bsz, 1))
    else:
        nblk = seq_len // WINDOW
        def blk(back, col):
            return lambda b, c: (b * nblk + jnp.maximum(c - back, 0), col)
        attn, a, conv_state = _mixers_ab(proj, lambda b, c: b * nblk + c, q_col, proj, proj,
                                         [blk(1, k_col), blk(1, v_col)], [blk(0, k_col), blk(0, v_col)],
                                         wts["sinks"][0], wts["conv_w_a"][0], prev_a,
                                         tq=WINDOW, band_mask=True, grid=(bsz, nblk))

    k_new = proj[:, k_col * KV_DIM:(k_col + 1) * KV_DIM].reshape(bsz, seq_len, N_KV, HEAD_DIM)
    v_new = proj[:, v_col * KV_DIM:(v_col + 1) * KV_DIM].reshape(bsz, seq_len, N_KV, HEAD_DIM)
    if sample:
        k_state = jnp.concatenate([swa_k_cache[0], k_new], axis=1)[:, seq_len:]
        v_state = jnp.concatenate([swa_v_cache[0], v_new], axis=1)[:, seq_len:]
    else:
        k_state, v_state = k_new[:, seq_len - WINDOW:], v_new[:, seq_len - WINDOW:]

    x = _matmul([a, attn], wts["w_out_ab"], 0, **MM_F32W, epilogue="residual", residual=x,
                name="out_ab")

    ffn_states = []

    def conv_ffn(x, layer, next_gain):
        h = _rmsnorm(x, wts["norm_ffn"][layer], BF16)
        prev, prev_layer = (ffn_cache, layer) if sample else (jnp.zeros((1, bsz, 2, 2 * D_FF), F32), 0)
        act, st_g, st_u, *maybe_wd = _ffn_up(h, wts["w_up_ffn"], wts["conv_w_ffn"], wts["conv_b_ffn"], layer,
                                             prev, prev_layer, seq_len,
                                             w_down=None if sample else wts["w_down_ffn"])
        if not sample:
            w_down_bf16[layer] = maybe_wd[0][None]
        ffn_states.append(jnp.concatenate([st_g, st_u], axis=-1))
        return _matmul([act], w_down_bf16[layer], 0, **MM_DOWN, epilogue="residual",
                       residual=x, next_gain=next_gain, name="ffn_down")

    x, xg, ss = conv_ffn(x, 0, wts["norm_mix"][1])

    z = _matmul([xg], wts["w_in_c"], 0, **MM_F32W, epilogue="gelu", row_ss=ss, name="in_c")
    cl = min(seq_len, GMLP_CHUNK)
    gated, *maybe_vn = _gmlp_gate(z, wts["ln_g_c"][0], wts["ln_b_c"][0], wts["w_s_c"][0], wts["b_s_c"][0].T,
                                  cl, want_vn=sample)
    gv_state = maybe_vn[0].reshape(1, bsz, seq_len, D_MODEL) if sample else None
    x = _matmul([gated], wts["w_out_c"], 0, **MM_F32W, epilogue="residual", residual=x,
                name="out_c")
    x = conv_ffn(x, 1, None)

    y = _rmsnorm(x, wts["norm_final"], F32)
    return (y.reshape(bsz, seq_len, D_MODEL), conv_state[None], k_state[None], v_state[None],
            jnp.stack(ffn_states), gv_state)


def kernel(x_prompt, x_sample, state_conv_a, cache_swa_k, cache_swa_v, state_ffn_conv, norm_mix, norm_ffn,
           norm_final, w_in_ab, conv_w_a, sinks, w_out_ab, w_in_c, ln_g_c, ln_b_c, w_s_c, b_s_c, w_out_c,
           w_up_ffn, conv_w_ffn, conv_b_ffn, w_down_ffn):
    wts = dict(norm_mix=norm_mix, norm_ffn=norm_ffn, norm_final=norm_final,
               w_in_ab=w_in_ab, conv_w_a=conv_w_a, sinks=sinks, w_out_ab=w_out_ab,
               w_in_c=w_in_c, ln_g_c=ln_g_c, ln_b_c=ln_b_c, w_s_c=w_s_c, b_s_c=b_s_c,
               w_out_c=w_out_c, w_up_ffn=w_up_ffn, conv_w_ffn=conv_w_ffn,
               conv_b_ffn=conv_b_ffn, w_down_ffn=w_down_ffn)
    w_down_bf16 = {}
    y_p, conv_p, k_p, v_p, ffn_p, _ = _trunk(x_prompt, None, None, None, None, wts, w_down_bf16)
    y_s, conv_s, k_s, v_s, ffn_s, gv_s = _trunk(x_sample, state_conv_a, cache_swa_k, cache_swa_v,
                                                state_ffn_conv, wts, w_down_bf16)
    return (y_p, y_s, conv_p, conv_s, k_p, k_s, v_p, v_s, ffn_p, ffn_s, gv_s)
```

```python
import functools

import numpy as np
import jax
import jax.numpy as jnp
from jax import lax
from jax.experimental import pallas as pl
from jax.experimental.pallas import tpu as pltpu

F32 = jnp.float32
BF16 = jnp.bfloat16

D_MODEL = 4096
CHUNK = 64
CONV_DIM = 2048
N_HEADS = 32
N_KV = 8
HEAD_DIM = 64
Q_DIM = N_HEADS * HEAD_DIM
KV_DIM = N_KV * HEAD_DIM
WINDOW = 128
GMLP_CHUNK = 128
GMLP_GROUPS = 16
GMLP_GDIM = D_MODEL // GMLP_GROUPS
D_FF = 11008
RMS_EPS = 1e-6
LN_EPS = 1e-5
NEG_INF = -1e30

V7X_VMEM_BYTES = 64 * 1024 * 1024
VMEM_LIMIT = V7X_VMEM_BYTES - 8 * 1024 * 1024

ROWS = 2048
SUB_ROWS = 512
EP_ROWS = 32
FFN_SUB_TILES = (512, 512, 512, 320, 192)
HALO = 8
FF_COLS = 256
N_FF_TILES = D_FF // FF_COLS
LANES = 128
SUBLANES = 8
BF16_SUBLANES = 16
N_SLAB = FF_COLS // LANES
SCR_PITCH = 2
MM_F32W = dict(tm=2048, tn=512, lhs_buffers=1)
MM_DOWN = dict(tm=512, tn=512, lhs_buffers=2)


def _params(*sem):
    return pltpu.CompilerParams(dimension_semantics=sem, vmem_limit_bytes=VMEM_LIMIT)


def _dot(a, b):
    return jnp.dot(a, b, preferred_element_type=F32)


def _as_bf16(x):
    return x if x.dtype == BF16 else x.astype(BF16)


def _rmsnorm_kernel(x_ref, g_ref, o_ref):
    x = x_ref[...]
    r = lax.rsqrt(jnp.mean(x * x, axis=-1, keepdims=True) + RMS_EPS)
    o_ref[...] = (x * r * g_ref[...]).astype(o_ref.dtype)


def _rmsnorm(x, g, out_dtype, tm=512):
    m, d = x.shape
    return pl.pallas_call(
        _rmsnorm_kernel,
        grid=(m // tm,),
        in_specs=[pl.BlockSpec((tm, d), lambda i: (i, 0)),
                  pl.BlockSpec((1, d), lambda i: (0, 0))],
        out_specs=pl.BlockSpec((tm, d), lambda i: (i, 0)),
        out_shape=jax.ShapeDtypeStruct((m, d), out_dtype),
        compiler_params=_params("parallel"),
        name="rmsnorm",
    )(x, g.reshape(1, d))


def _gelu(x):
    return 0.5 * x * (1.0 + lax.erf(x * np.float32(np.sqrt(0.5))))


def _row_scale(ss):
    ms = jnp.sum(ss, axis=-1, keepdims=True) * (1.0 / D_MODEL)
    return jnp.broadcast_to(lax.rsqrt(ms + RMS_EPS), ss.shape)


def _per_lane_tile(fn, x):
    return jnp.concatenate([fn(x[:, k * LANES:(k + 1) * LANES]) for k in range(x.shape[1] // LANES)], axis=1)


def _lane_partial_sumsq(x):
    sq = x * x
    part = sq[:, 0:LANES]
    for k in range(1, sq.shape[1] // LANES):
        part = part + sq[:, k * LANES:(k + 1) * LANES]
    return part


def _mm_kernel(*refs, n_lhs, epilogue, tm, scale_rows, emit_norm):
    refs = list(refs)
    lhs, w_refs = refs[:n_lhs], refs[n_lhs:2 * n_lhs]
    rest = refs[2 * n_lhs:]
    ss_in = rest.pop(0) if scale_rows else None
    res_ref = rest.pop(0) if epilogue == "residual" else None
    gain_ref = rest.pop(0) if emit_norm else None
    o_ref = rest.pop(0)
    xg_ref, ss_out = (rest.pop(0), rest.pop(0)) if emit_norm else (None, None)
    r_scr = rest.pop(0) if scale_rows else None
    first_col = pl.program_id(1) == 0

    if scale_rows:
        @pl.when(first_col)
        def _():
            r_scr[...] = _row_scale(ss_in[...])
    if emit_norm:
        @pl.when(first_col)
        def _():
            ss_out[...] = jnp.zeros(ss_out.shape, F32)

    ws = [_as_bf16(w[...]) for w in w_refs]
    for r in range(tm // SUB_ROWS):
        rows = slice(r * SUB_ROWS, (r + 1) * SUB_ROWS)
        acc = _dot(lhs[0][rows, :], ws[0])
        for l in range(1, n_lhs):
            acc = acc + _dot(lhs[l][rows, :], ws[l])
        if scale_rows:
            scale = r_scr[rows, :]
            acc = _per_lane_tile(lambda t: t * scale, acc)
        if epilogue == "residual":
            acc = res_ref[rows, :] + acc
        elif epilogue == "gelu":
            acc = _gelu(acc)
        o_ref[rows, :] = acc
        if emit_norm:
            xg_ref[rows, :] = (acc * gain_ref[...]).astype(BF16)
            ss_out[rows, :] += _lane_partial_sumsq(acc)


def _matmul(lhs_list, w, layer, *, tm, tn, lhs_buffers, epilogue="none", residual=None, row_ss=None,
            next_gain=None, name):
    m = lhs_list[0].shape[0]
    n = w.shape[2]
    in_specs, args = [], []
    for l in lhs_list:
        in_specs.append(pl.BlockSpec((tm, l.shape[1]), lambda i, j: (i, 0),
                                     pipeline_mode=pl.Buffered(lhs_buffers)))
        args.append(l)
    row_off = 0
    for l in lhs_list:
        k = l.shape[1]
        in_specs.append(pl.BlockSpec((None, k, tn),
                                     functools.partial(lambda i, j, rb: (layer, rb, j), rb=row_off // k)))
        args.append(w)
        row_off += k
    if row_ss is not None:
        in_specs.append(pl.BlockSpec((tm, LANES), lambda i, j: (i, 0)))
        args.append(row_ss)
    if residual is not None:
        in_specs.append(pl.BlockSpec((tm, tn), lambda i, j: (i, j)))
        args.append(residual)
    out_specs = [pl.BlockSpec((tm, tn), lambda i, j: (i, j))]
    out_shape = [jax.ShapeDtypeStruct((m, n), F32)]
    if next_gain is not None:
        in_specs.append(pl.BlockSpec((1, tn), lambda i, j: (0, j)))
        args.append(next_gain.reshape(1, n))
        out_specs += [pl.BlockSpec((tm, tn), lambda i, j: (i, j)), pl.BlockSpec((tm, LANES), lambda i, j: (i, 0))]
        out_shape += [jax.ShapeDtypeStruct((m, n), BF16), jax.ShapeDtypeStruct((m, LANES), F32)]
    outs = pl.pallas_call(
        functools.partial(_mm_kernel, n_lhs=len(lhs_list), epilogue=epilogue, tm=tm,
                          scale_rows=row_ss is not None, emit_norm=next_gain is not None),
        grid=(m // tm, n // tn),
        in_specs=in_specs,
        out_specs=out_specs,
        out_shape=out_shape,
        scratch_shapes=[pltpu.VMEM((tm, LANES), F32)] if row_ss is not None else [],
        compiler_params=_params("parallel", "arbitrary"),
        name=name,
    )(*args)
    return outs if next_gain is not None else outs[0]


def _causal_conv3(x, prev, w, seq_len):
    rows, c = x.shape
    nseq = rows // seq_len
    t = lax.broadcasted_iota(jnp.int32, (rows, 1), 0) & (seq_len - 1)
    p0 = jnp.broadcast_to(prev[:, 0:1, :], (nseq, seq_len, c)).reshape(rows, c)
    p1 = jnp.broadcast_to(prev[:, 1:2, :], (nseq, seq_len, c)).reshape(rows, c)
    x1 = jnp.where(t == 0, p1, pltpu.roll(x, 1, 0))
    x2 = jnp.where(t == 0, p0, jnp.where(t == 1, p1, pltpu.roll(x, 2, 0)))
    return x2 * w[0:1, :] + x1 * w[1:2, :] + x * w[2:3, :]


def _alibi_slopes():
    exps = np.arange(1, N_HEADS + 1, dtype=np.float32) * (8.0 / N_HEADS)
    return np.power(2.0, -exps).astype(np.float32)


def _attn_kernel(sink_ref, q_ref, kh_ref, kn_ref, vh_ref, vn_ref, gb_ref, gc_ref, hv_ref, cw_ref, prev_ref,
                 o_ref, a_ref, cs_ref, o_acc, bias_scr, sc_scr, w_scr, carry_scr, *maybe_cap_scr, tq):
    @pl.when(pl.program_id(1) == 0)
    def _():
        carry_scr[...] = prev_ref[0]

    xin = gc_ref[...] * hv_ref[...]
    conv = _causal_conv3(xin, carry_scr[...][None], cw_ref[...], tq)
    a_ref[...] = (gb_ref[...] * conv).astype(a_ref.dtype)
    carry_scr[...] = xin[tq - 2:tq, :]
    cs_ref[0] = xin[tq - 2:tq, :]

    nk = kh_ref.shape[0] + kn_ref.shape[0]
    group = N_HEADS // N_KV
    rows = group * tq
    slopes = _alibi_slopes()

    def per_head(shape, head_in_group, vals):
        out = jnp.full(shape, vals[group - 1], F32)
        for g in range(group - 2, -1, -1):
            out = jnp.where(head_in_group == g, vals[g], out)
        return out

    @pl.when((pl.program_id(0) == 0) & (pl.program_id(1) == 0))
    def _():
        s = lax.broadcasted_iota(jnp.int32, (nk, rows), 0)
        col = lax.broadcasted_iota(jnp.int32, (nk, rows), 1)
        t = col & (tq - 1)
        neg_dist = -jnp.abs(t + (WINDOW - s)).astype(F32)
        for kh in range(N_KV):
            slope = per_head((nk, rows), col // tq, [float(slopes[kh * group + g]) for g in range(group)])
            bias_scr[kh] = slope * neg_dist
        for cap_scr in maybe_cap_scr:
            q_chunk, k_chunk = t // CHUNK, s // CHUNK
            in_band = (k_chunk >= q_chunk) & (k_chunk <= q_chunk + WINDOW // CHUNK)
            for first_block in range(2):
                visible = in_band & (k_chunk >= first_block * (WINDOW // CHUNK))
                cap_scr[first_block] = jnp.where(visible, jnp.inf, NEG_INF)

    first_block = jnp.where(pl.program_id(1) == 0, 1, 0)
    head_of_lane = lax.broadcasted_iota(jnp.int32, (1, rows), 1) // tq

    for kh in range(N_KV):
        cols = slice(kh * HEAD_DIM, (kh + 1) * HEAD_DIM)
        k_all = jnp.concatenate([kh_ref[:, cols], kn_ref[:, cols]], axis=0).astype(BF16)
        q = jnp.concatenate([q_ref[:, h * HEAD_DIM:(h + 1) * HEAD_DIM]
                             for h in range(kh * group, (kh + 1) * group)], axis=0)
        q = (q * (HEAD_DIM ** -0.5)).astype(BF16)
        sc_scr[kh] = lax.dot_general(k_all, q, (((1,), (1,)), ((), ())), preferred_element_type=F32)
    for kh in range(N_KV):
        sink_row = per_head((1, rows), head_of_lane, [sink_ref[kh * group + g] for g in range(group)])
        for c0 in range(0, rows, LANES):
            lanes = slice(c0, c0 + LANES)
            s = sc_scr[kh, :, lanes] + bias_scr[kh, :, lanes]
            for cap_scr in maybe_cap_scr:
                s = jnp.minimum(s, cap_scr[first_block, :, lanes])
            sink = sink_row[:, lanes]
            m = jnp.maximum(jnp.max(s, axis=0, keepdims=True), sink)
            p = jnp.exp(s - m)
            denom = jnp.sum(p, axis=0, keepdims=True) + jnp.exp(sink - m)
            w_scr[kh, :, lanes] = (p * (1.0 / denom)).astype(BF16)
    for kh in range(N_KV):
        cols = slice(kh * HEAD_DIM, (kh + 1) * HEAD_DIM)
        v_all = jnp.concatenate([vh_ref[:, cols], vn_ref[:, cols]], axis=0).astype(BF16)
        out = lax.dot_general(w_scr[kh], v_all, (((0,), (0,)), ((), ())), preferred_element_type=F32)
        for g in range(group):
            h = kh * group + g
            o_acc[:, h * HEAD_DIM:(h + 1) * HEAD_DIM] = out[g * tq:(g + 1) * tq, :]
    o_ref[...] = o_acc[...].astype(o_ref.dtype)


def _mixers_ab(proj, row_map, q_col, k_hist, v_hist, hist_maps, new_maps, sinks, conv_w, conv_prev,
               *, tq, band_mask, grid):
    m = proj.shape[0]
    nk, rows = WINDOW + tq, N_HEADS // N_KV * tq

    def proj_cols(col):
        return lambda b, c: (row_map(b, c), col)

    conv_specs = [pl.BlockSpec((tq, CONV_DIM), proj_cols(col)) for col in range(3)]
    return pl.pallas_call(
        functools.partial(_attn_kernel, tq=tq),
        grid=grid,
        in_specs=[pl.BlockSpec(memory_space=pltpu.SMEM),
                  pl.BlockSpec((tq, Q_DIM), proj_cols(q_col)),
                  pl.BlockSpec((WINDOW, KV_DIM), hist_maps[0]), pl.BlockSpec((tq, KV_DIM), new_maps[0]),
                  pl.BlockSpec((WINDOW, KV_DIM), hist_maps[1]), pl.BlockSpec((tq, KV_DIM), new_maps[1])]
                 + conv_specs
                 + [pl.BlockSpec((3, CONV_DIM), lambda b, c: (0, 0)),
                    pl.BlockSpec((1, 2, CONV_DIM), lambda b, c: (b, 0, 0))],
        out_specs=[pl.BlockSpec((tq, Q_DIM), proj_cols(0)),
                   pl.BlockSpec((tq, CONV_DIM), proj_cols(0)),
                   pl.BlockSpec((1, 2, CONV_DIM), lambda b, c: (b, 0, 0))],
        out_shape=[jax.ShapeDtypeStruct((m, Q_DIM), BF16),
                   jax.ShapeDtypeStruct((m, CONV_DIM), BF16),
                   jax.ShapeDtypeStruct((grid[0], 2, CONV_DIM), F32)],
        scratch_shapes=[pltpu.VMEM((tq, Q_DIM), F32)]
                       + [pltpu.VMEM((N_KV, nk, rows), F32)] * 2
                       + [pltpu.VMEM((N_KV, nk, rows), BF16)]
                       + [pltpu.VMEM((2, CONV_DIM), F32)]
                       + ([pltpu.VMEM((2, nk, rows), F32)] if band_mask else []),
        compiler_params=_params("arbitrary", "arbitrary"),
        name="swa",
    )(sinks, proj, k_hist, proj, v_hist, proj, proj, proj, proj, conv_w, conv_prev)


def _gmlp_kernel(u_ref, v_ref, lng_ref, lnb_ref, ws_ref, bs_ref, o_ref, *maybe_vn_ref, cl):
    v = v_ref[...]
    mu = jnp.mean(v, axis=-1, keepdims=True)
    var = jnp.mean(jnp.square(v - mu), axis=-1, keepdims=True)
    vn = (v - mu) * lax.rsqrt(var + LN_EPS) * lng_ref[...] + lnb_ref[...]
    for vn_ref in maybe_vn_ref:
        vn_ref[...] = vn
    row = lax.broadcasted_iota(jnp.int32, (cl, cl), 0)
    col = lax.broadcasted_iota(jnp.int32, (cl, cl), 1)
    lower = row >= col
    for g in range(GMLP_GROUPS):
        cols = slice(g * GMLP_GDIM, (g + 1) * GMLP_GDIM)
        ws = jnp.where(lower, ws_ref[g, 0:cl, 0:cl], 0.0).astype(BF16)
        s = _dot(ws, vn[:, cols].astype(BF16)) + bs_ref[0:cl, g:g + 1]
        o_ref[:, cols] = (u_ref[:, cols] * s).astype(o_ref.dtype)


def _gmlp_gate(z, ln_g, ln_b, w_s, b_s_t, cl, want_vn):
    m = z.shape[0]
    n_out = 2 if want_vn else 1
    return pl.pallas_call(
        functools.partial(_gmlp_kernel, cl=cl),
        grid=(m // cl,),
        in_specs=[pl.BlockSpec((cl, D_MODEL), lambda i: (i, 0)),
                  pl.BlockSpec((cl, D_MODEL), lambda i: (i, 1)),
                  pl.BlockSpec((1, D_MODEL), lambda i: (0, 0)),
                  pl.BlockSpec((1, D_MODEL), lambda i: (0, 0)),
                  pl.BlockSpec((GMLP_GROUPS, GMLP_CHUNK, GMLP_CHUNK), lambda i: (0, 0, 0)),
                  pl.BlockSpec((GMLP_CHUNK, GMLP_GROUPS), lambda i: (0, 0))],
        out_specs=[pl.BlockSpec((cl, D_MODEL), lambda i: (i, 0)),
                   pl.BlockSpec((cl, D_MODEL), lambda i: (i, 0))][:n_out],
        out_shape=[jax.ShapeDtypeStruct((m, D_MODEL), BF16),
                   jax.ShapeDtypeStruct((m, D_MODEL), F32)][:n_out],
        compiler_params=_params("parallel"),
        name="gmlp_gate",
    )(z, z, ln_g.reshape(1, -1), ln_b.reshape(1, -1), w_s, b_s_t)


def _ffn_up_kernel(h_ref, wg_ref, wu_ref, cwg_ref, cwu_ref, bg_ref, bu_ref, pg_ref, pu_ref, *rest,
                   seq_len, cast_w_down):
    rest = list(rest)
    wd_ref = rest.pop(0) if cast_w_down else None
    act_ref, sg_ref, su_ref = rest.pop(0), rest.pop(0), rest.pop(0)
    wd_bf16_ref = rest.pop(0) if cast_w_down else None
    ug_scr, uu_scr = rest
    if cast_w_down:
        wd_bf16_ref[...] = wd_ref[...].astype(BF16)

    wg, wu = _as_bf16(wg_ref[...]), _as_bf16(wu_ref[...])
    ug_scr[:, 0:SCR_PITCH * HALO, :] = jnp.zeros((N_SLAB, SCR_PITCH * HALO, LANES), F32)
    uu_scr[:, 0:SCR_PITCH * HALO, :] = jnp.zeros((N_SLAB, SCR_PITCH * HALO, LANES), F32)
    row = lax.broadcasted_iota(jnp.int32, (EP_ROWS, 1), 0)

    def rows_at(scr, s, t, n):
        return scr[s, pl.ds(SCR_PITCH * (HALO + t), n, stride=SCR_PITCH), :]

    def conv(scr, prev_ref, cw_ref, b_ref, s, start, z):
        lanes = slice(s * LANES, (s + 1) * LANES)
        x0, x1, x2 = (rows_at(scr, s, start - back, EP_ROWS) for back in range(3))
        if start % seq_len == 0:
            prev = prev_ref[start // seq_len]
            x1 = jnp.where(row == 0, prev[1:2, lanes], x1)
            x2 = jnp.where(row == 0, prev[0:1, lanes], jnp.where(row == 1, prev[1:2, lanes], x2))

        def tap(x, w):
            return (x.reshape(EP_ROWS // SUBLANES, SUBLANES, LANES) * (w + z)[None]).reshape(EP_ROWS, LANES)

        return (tap(x2, cw_ref[0:1, lanes]) + tap(x1, cw_ref[1:2, lanes]) + tap(x0, cw_ref[2:3, lanes])
                + b_ref[:, lanes])

    zs = [jnp.zeros((SUBLANES, LANES), F32)] * N_SLAB

    row0 = 0
    for sub_rows in FFN_SUB_TILES:
        h = h_ref[row0:row0 + sub_rows, :]
        up_g, up_u = _dot(h, wg), _dot(h, wu)
        for s in range(N_SLAB):
            dst = pl.ds(SCR_PITCH * (HALO + row0), sub_rows, stride=SCR_PITCH)
            ug_scr[s, dst, :] = up_g[:, s * LANES:(s + 1) * LANES]
            uu_scr[s, dst, :] = up_u[:, s * LANES:(s + 1) * LANES]
        chunk_starts = range(row0, row0 + sub_rows, EP_ROWS)
        row0 += sub_rows
        for start in chunk_starts:
            end = start + EP_ROWS
            for s in range(N_SLAB):
                lanes = slice(s * LANES, (s + 1) * LANES)
                cg = conv(ug_scr, pg_ref, cwg_ref, bg_ref, s, start, zs[s])
                cu = conv(uu_scr, pu_ref, cwu_ref, bu_ref, s, start, zs[s])
                act = cg * jax.nn.sigmoid(cg) * cu
                act_ref[start:end, lanes] = act.astype(act_ref.dtype)
                bits = lax.bitcast_convert_type(act[0:SUBLANES, :], jnp.uint32)
                zs[s] = ((bits >> 16) >> 16).astype(F32)
                if end % seq_len == 0:
                    sg_ref[end // seq_len - 1, :, lanes] = rows_at(ug_scr, s, end - 2, 2)
                    su_ref[end // seq_len - 1, :, lanes] = rows_at(uu_scr, s, end - 2, 2)


def _ffn_up(h, w_up, conv_w, conv_b, layer, prev, prev_layer, seq_len, w_down=None):
    m = h.shape[0]
    assert sum(FFN_SUB_TILES) == ROWS and all(t % EP_ROWS == 0 for t in FFN_SUB_TILES)
    assert min(seq_len, ROWS) % EP_ROWS == 0 and ROWS % seq_len == 0
    nseq_tile = ROWS // seq_len
    nj = N_FF_TILES
    st_spec_g = pl.BlockSpec((None, nseq_tile, 2, FF_COLS), lambda i, j: (prev_layer, i, 0, j))
    st_spec_u = pl.BlockSpec((None, nseq_tile, 2, FF_COLS), lambda i, j: (prev_layer, i, 0, j + nj))
    conv_b = conv_b.reshape(conv_b.shape[0], 1, -1)
    in_specs = [pl.BlockSpec((ROWS, D_MODEL), lambda i, j: (i, 0), pipeline_mode=pl.Buffered(1)),
                pl.BlockSpec((None, D_MODEL, FF_COLS), lambda i, j: (layer, 0, j)),
                pl.BlockSpec((None, D_MODEL, FF_COLS), lambda i, j: (layer, 0, j + nj)),
                pl.BlockSpec((None, 3, FF_COLS), lambda i, j: (layer, 0, j)),
                pl.BlockSpec((None, 3, FF_COLS), lambda i, j: (layer, 0, j + nj)),
                pl.BlockSpec((None, 1, FF_COLS), lambda i, j: (layer, 0, j)),
                pl.BlockSpec((None, 1, FF_COLS), lambda i, j: (layer, 0, j + nj)),
                st_spec_g, st_spec_u]
    args = [h, w_up, w_up, conv_w, conv_w, conv_b, conv_b, prev, prev]
    out_specs = [pl.BlockSpec((ROWS, FF_COLS), lambda i, j: (i, j)),
                 pl.BlockSpec((nseq_tile, 2, FF_COLS), lambda i, j: (i, 0, j)),
                 pl.BlockSpec((nseq_tile, 2, FF_COLS), lambda i, j: (i, 0, j))]
    out_shape = [jax.ShapeDtypeStruct((m, D_FF), BF16),
                 jax.ShapeDtypeStruct((m // seq_len, 2, D_FF), F32),
                 jax.ShapeDtypeStruct((m // seq_len, 2, D_FF), F32)]
    if w_down is not None:
        n_steps = (m // ROWS) * nj
        cast_rows = D_FF // n_steps
        assert cast_rows * n_steps == D_FF and cast_rows % BF16_SUBLANES == 0
        in_specs.append(pl.BlockSpec((None, cast_rows, D_MODEL), lambda i, j: (layer, i * nj + j, 0)))
        args.append(w_down)
        out_specs.append(pl.BlockSpec((cast_rows, D_MODEL), lambda i, j: (i * nj + j, 0)))
        out_shape.append(jax.ShapeDtypeStruct((D_FF, D_MODEL), BF16))
    return pl.pallas_call(
        functools.partial(_ffn_up_kernel, seq_len=seq_len, cast_w_down=w_down is not None),
        grid=(m // ROWS, nj),
        in_specs=in_specs,
        out_specs=out_specs,
        out_shape=out_shape,
        scratch_shapes=[pltpu.VMEM((N_SLAB, SCR_PITCH * (HALO + ROWS), LANES), F32)] * 2,
        compiler_params=_params("parallel", "arbitrary"),
        name="ffn_up",
    )(*args)


def _trunk(x3, conv_a_cache, swa_k_cache, swa_v_cache, ffn_cache, wts, w_down_bf16):
    bsz, seq_len, _ = x3.shape
    m = bsz * seq_len
    sample = ffn_cache is not None
    x = x3.reshape(m, D_MODEL)

    h = _rmsnorm(x, wts["norm_mix"][0], BF16)
    proj = _matmul([h], wts["w_in_ab"], 0, **MM_F32W, name="in_ab")
    prev_a = conv_a_cache[0] if sample else jnp.zeros((bsz, 2, CONV_DIM), F32)

    q_col = 3 * CONV_DIM // Q_DIM
    k_col = (3 * CONV_DIM + Q_DIM) // KV_DIM
    v_col = k_col + 1
    if sample:
        k_hist = swa_k_cache[0].reshape(bsz * WINDOW, KV_DIM)
        v_hist = swa_v_cache[0].reshape(bsz * WINDOW, KV_DIM)
        hist = [lambda b, c: (b, 0)] * 2
        new = [lambda b, c: (b, k_col), lambda b, c: (b, v_col)]
        attn, a, conv_state = _mixers_ab(proj, lambda b, c: b, q_col, k_hist, v_hist, hist, new,
                                         wts["sinks"][0], wts["conv_w_a"][0], prev_a,
                                         tq=seq_len, band_mask=False, grid=(bsz, 1))
    else:
        nblk = seq_len // WINDOW
        def blk(back, col):
            return lambda b, c: (b * nblk + jnp.maximum(c - back, 0), col)
        attn, a, conv_state = _mixers_ab(proj, lambda b, c: b * nblk + c, q_col, proj, proj,
                                         [blk(1, k_col), blk(1, v_col)], [blk(0, k_col), blk(0, v_col)],
                                         wts["sinks"][0], wts["conv_w_a"][0], prev_a,
                                         tq=WINDOW, band_mask=True, grid=(bsz, nblk))

    keep = min(seq_len, WINDOW)
    kv_tail = proj.reshape(bsz, seq_len, proj.shape[1])[:, seq_len - keep:, k_col * KV_DIM:(v_col + 1) * KV_DIM]
    k_state = kv_tail[:, :, :KV_DIM].reshape(bsz, keep, N_KV, HEAD_DIM)
    v_state = kv_tail[:, :, KV_DIM:].reshape(bsz, keep, N_KV, HEAD_DIM)
    if sample:
        k_state = jnp.concatenate([swa_k_cache[0][:, keep:], k_state], axis=1)
        v_state = jnp.concatenate([swa_v_cache[0][:, keep:], v_state], axis=1)

    x = _matmul([a, attn], wts["w_out_ab"], 0, **MM_F32W, epilogue="residual", residual=x,
                name="out_ab")

    ffn_states = []

    def conv_ffn(x, layer, next_gain):
        h = _rmsnorm(x, wts["norm_ffn"][layer], BF16)
        prev, prev_layer = (ffn_cache, layer) if sample else (jnp.zeros((1, bsz, 2, 2 * D_FF), F32), 0)
        act, st_g, st_u, *maybe_wd = _ffn_up(h, wts["w_up_ffn"], wts["conv_w_ffn"], wts["conv_b_ffn"], layer,
                                             prev, prev_layer, seq_len,
                                             w_down=None if sample else wts["w_down_ffn"])
        if not sample:
            w_down_bf16[layer] = maybe_wd[0][None]
        ffn_states.append(jnp.concatenate([st_g, st_u], axis=-1))
        return _matmul([act], w_down_bf16[layer], 0, **MM_DOWN, epilogue="residual",
                       residual=x, next_gain=next_gain, name="ffn_down")

    x, xg, ss = conv_ffn(x, 0, wts["norm_mix"][1])

    z = _matmul([xg], wts["w_in_c"], 0, **MM_F32W, epilogue="gelu", row_ss=ss, name="in_c")
    cl = min(seq_len, GMLP_CHUNK)
    gated, *maybe_vn = _gmlp_gate(z, wts["ln_g_c"][0], wts["ln_b_c"][0], wts["w_s_c"][0], wts["b_s_c"][0].T,
                                  cl, want_vn=sample)
    gv_state = maybe_vn[0].reshape(1, bsz, seq_len, D_MODEL) if sample else None
    x = _matmul([gated], wts["w_out_c"], 0, **MM_F32W, epilogue="residual", residual=x,
                name="out_c")
    x = conv_ffn(x, 1, None)

    y = _rmsnorm(x, wts["norm_final"], F32)
    return (y.reshape(bsz, seq_len, D_MODEL), conv_state[None], k_state[None], v_state[None],
            jnp.stack(ffn_states), gv_state)


def kernel(x_prompt, x_sample, state_conv_a, cache_swa_k, cache_swa_v, state_ffn_conv, norm_mix, norm_ffn,
           norm_final, w_in_ab, conv_w_a, sinks, w_out_ab, w_in_c, ln_g_c, ln_b_c, w_s_c, b_s_c, w_out_c,
           w_up_ffn, conv_w_ffn, conv_b_ffn, w_down_ffn):
    wts = dict(norm_mix=norm_mix, norm_ffn=norm_ffn, norm_final=norm_final,
               w_in_ab=w_in_ab, conv_w_a=conv_w_a, sinks=sinks, w_out_ab=w_out_ab,
               w_in_c=w_in_c, ln_g_c=ln_g_c, ln_b_c=ln_b_c, w_s_c=w_s_c, b_s_c=b_s_c,
               w_out_c=w_out_c, w_up_ffn=w_up_ffn, conv_w_ffn=conv_w_ffn,
               conv_b_ffn=conv_b_ffn, w_down_ffn=w_down_ffn)
    w_down_bf16 = {}
    y_p, conv_p, k_p, v_p, ffn_p, _ = _trunk(x_prompt, None, None, None, None, wts, w_down_bf16)
    y_s, conv_s, k_s, v_s, ffn_s, gv_s = _trunk(x_sample, state_conv_a, cache_swa_k, cache_swa_v,
                                                state_ffn_conv, wts, w_down_bf16)
    return (y_p, y_s, conv_p, conv_s, k_p, k_s, v_p, v_s, ffn_p, ffn_s, gv_s)
```

```python
import functools

import numpy as np
import jax
import jax.numpy as jnp
from jax import lax
from jax.experimental import pallas as pl
from jax.experimental.pallas import tpu as pltpu

F32 = jnp.float32
BF16 = jnp.bfloat16

D_MODEL = 4096
CHUNK = 64
CONV_DIM = 2048
N_HEADS = 32
N_KV = 8
HEAD_DIM = 64
Q_DIM = N_HEADS * HEAD_DIM
KV_DIM = N_KV * HEAD_DIM
WINDOW = 128
GMLP_CHUNK = 128
GMLP_GROUPS = 16
GMLP_GDIM = D_MODEL // GMLP_GROUPS
D_FF = 11008
RMS_EPS = 1e-6
LN_EPS = 1e-5
NEG_INF = -1e30

V7X_VMEM_BYTES = 64 * 1024 * 1024
VMEM_LIMIT = V7X_VMEM_BYTES - 8 * 1024 * 1024

ROWS = 2048
SUB_ROWS = 512
EP_ROWS = 32
FFN_SUB_TILES = (512, 512, 512, 320, 192)
HALO = 8
FF_COLS = 256
N_FF_TILES = D_FF // FF_COLS
LANES = 128
SUBLANES = 8
BF16_SUBLANES = 16
N_SLAB = FF_COLS // LANES
SCR_PITCH = 2
MM_F32W = dict(tm=2048, tn=512, lhs_buffers=1)
MM_DOWN = dict(tm=512, tn=512, lhs_buffers=2)


def _params(*sem):
    return pltpu.CompilerParams(dimension_semantics=sem, vmem_limit_bytes=VMEM_LIMIT)


def _dot(a, b):
    return jnp.dot(a, b, preferred_element_type=F32)


def _as_bf16(x):
    return x if x.dtype == BF16 else x.astype(BF16)


def _rmsnorm_kernel(x_ref, g_ref, o_ref):
    x = x_ref[...]
    r = lax.rsqrt(jnp.mean(x * x, axis=-1, keepdims=True) + RMS_EPS)
    o_ref[...] = (x * r * g_ref[...]).astype(o_ref.dtype)


def _rmsnorm(x, g, out_dtype, tm=512):
    m, d = x.shape
    return pl.pallas_call(
        _rmsnorm_kernel,
        grid=(m // tm,),
        in_specs=[pl.BlockSpec((tm, d), lambda i: (i, 0)),
                  pl.BlockSpec((1, d), lambda i: (0, 0))],
        out_specs=pl.BlockSpec((tm, d), lambda i: (i, 0)),
        out_shape=jax.ShapeDtypeStruct((m, d), out_dtype),
        compiler_params=_params("parallel"),
        name="rmsnorm",
    )(x, g.reshape(1, d))


def _gelu(x):
    return 0.5 * x * (1.0 + lax.erf(x * np.float32(np.sqrt(0.5))))


def _row_scale(ss):
    ms = jnp.sum(ss, axis=-1, keepdims=True) * (1.0 / D_MODEL)
    return jnp.broadcast_to(lax.rsqrt(ms + RMS_EPS), ss.shape)


def _per_lane_tile(fn, x):
    return jnp.concatenate([fn(x[:, k * LANES:(k + 1) * LANES]) for k in range(x.shape[1] // LANES)], axis=1)


def _lane_partial_sumsq(x):
    sq = x * x
    part = sq[:, 0:LANES]
    for k in range(1, sq.shape[1] // LANES):
        part = part + sq[:, k * LANES:(k + 1) * LANES]
    return part


def _mm_kernel(*refs, n_lhs, epilogue, tm, scale_rows, emit_norm):
    refs = list(refs)
    lhs, w_refs = refs[:n_lhs], refs[n_lhs:2 * n_lhs]
    rest = refs[2 * n_lhs:]
    ss_in = rest.pop(0) if scale_rows else None
    res_ref = rest.pop(0) if epilogue == "residual" else None
    gain_ref = rest.pop(0) if emit_norm else None
    o_ref = rest.pop(0)
    xg_ref, ss_out = (rest.pop(0), rest.pop(0)) if emit_norm else (None, None)
    r_scr = rest.pop(0) if scale_rows else None
    first_col = pl.program_id(1) == 0

    if scale_rows:
        @pl.when(first_col)
        def _():
            r_scr[...] = _row_scale(ss_in[...])
    if emit_norm:
        @pl.when(first_col)
        def _():
            ss_out[...] = jnp.zeros(ss_out.shape, F32)

    ws = [_as_bf16(w[...]) for w in w_refs]
    for r in range(tm // SUB_ROWS):
        rows = slice(r * SUB_ROWS, (r + 1) * SUB_ROWS)
        acc = _dot(lhs[0][rows, :], ws[0])
        for l in range(1, n_lhs):
            acc = acc + _dot(lhs[l][rows, :], ws[l])
        if scale_rows:
            scale = r_scr[rows, :]
            acc = _per_lane_tile(lambda t: t * scale, acc)
        if epilogue == "residual":
            acc = res_ref[rows, :] + acc
        elif epilogue == "gelu":
            acc = _gelu(acc)
        o_ref[rows, :] = acc
        if emit_norm:
            xg_ref[rows, :] = (acc * gain_ref[...]).astype(BF16)
            ss_out[rows, :] += _lane_partial_sumsq(acc)


def _matmul(lhs_list, w, layer, *, tm, tn, lhs_buffers, epilogue="none", residual=None, row_ss=None,
            next_gain=None, name):
    m = lhs_list[0].shape[0]
    n = w.shape[2]
    in_specs, args = [], []
    for l in lhs_list:
        in_specs.append(pl.BlockSpec((tm, l.shape[1]), lambda i, j: (i, 0),
                                     pipeline_mode=pl.Buffered(lhs_buffers)))
        args.append(l)
    row_off = 0
    for l in lhs_list:
        k = l.shape[1]
        in_specs.append(pl.BlockSpec((None, k, tn),
                                     functools.partial(lambda i, j, rb: (layer, rb, j), rb=row_off // k)))
        args.append(w)
        row_off += k
    if row_ss is not None:
        in_specs.append(pl.BlockSpec((tm, LANES), lambda i, j: (i, 0)))
        args.append(row_ss)
    if residual is not None:
        in_specs.append(pl.BlockSpec((tm, tn), lambda i, j: (i, j)))
        args.append(residual)
    out_specs = [pl.BlockSpec((tm, tn), lambda i, j: (i, j))]
    out_shape = [jax.ShapeDtypeStruct((m, n), F32)]
    if next_gain is not None:
        in_specs.append(pl.BlockSpec((1, tn), lambda i, j: (0, j)))
        args.append(next_gain.reshape(1, n))
        out_specs += [pl.BlockSpec((tm, tn), lambda i, j: (i, j)), pl.BlockSpec((tm, LANES), lambda i, j: (i, 0))]
        out_shape += [jax.ShapeDtypeStruct((m, n), BF16), jax.ShapeDtypeStruct((m, LANES), F32)]
    outs = pl.pallas_call(
        functools.partial(_mm_kernel, n_lhs=len(lhs_list), epilogue=epilogue, tm=tm,
                          scale_rows=row_ss is not None, emit_norm=next_gain is not None),
        grid=(m // tm, n // tn),
        in_specs=in_specs,
        out_specs=out_specs,
        out_shape=out_shape,
        scratch_shapes=[pltpu.VMEM((tm, LANES), F32)] if row_ss is not None else [],
        compiler_params=_params("parallel", "arbitrary"),
        name=name,
    )(*args)
    return outs if next_gain is not None else outs[0]


def _causal_conv3(x, prev, w, seq_len):
    rows, c = x.shape
    nseq = rows // seq_len
    t = lax.broadcasted_iota(jnp.int32, (rows, 1), 0) & (seq_len - 1)
    p0 = jnp.broadcast_to(prev[:, 0:1, :], (nseq, seq_len, c)).reshape(rows, c)
    p1 = jnp.broadcast_to(prev[:, 1:2, :], (nseq, seq_len, c)).reshape(rows, c)
    x1 = jnp.where(t == 0, p1, pltpu.roll(x, 1, 0))
    x2 = jnp.where(t == 0, p0, jnp.where(t == 1, p1, pltpu.roll(x, 2, 0)))
    return x2 * w[0:1, :] + x1 * w[1:2, :] + x * w[2:3, :]


def _alibi_slopes():
    exps = np.arange(1, N_HEADS + 1, dtype=np.float32) * (8.0 / N_HEADS)
    return np.power(2.0, -exps).astype(np.float32)


def _attn_kernel(sink_ref, q_ref, kh_ref, kn_ref, vh_ref, vn_ref, gb_ref, gc_ref, hv_ref, cw_ref, prev_ref,
                 o_ref, a_ref, cs_ref, ks_ref, vs_ref, o_acc, bias_scr, sc_scr, w_scr, carry_scr,
                 *maybe_cap_scr, tq):
    @pl.when(pl.program_id(1) == 0)
    def _():
        carry_scr[...] = prev_ref[0]

    xin = gc_ref[...] * hv_ref[...]
    conv = _causal_conv3(xin, carry_scr[...][None], cw_ref[...], tq)
    a_ref[...] = (gb_ref[...] * conv).astype(a_ref.dtype)
    carry_scr[...] = xin[tq - 2:tq, :]
    cs_ref[0] = xin[tq - 2:tq, :]

    old = WINDOW - tq
    if old > 0:
        ks_ref[0:old, :] = kh_ref[tq:WINDOW, :]
        vs_ref[0:old, :] = vh_ref[tq:WINDOW, :]
    ks_ref[old:WINDOW, :] = kn_ref[...]
    vs_ref[old:WINDOW, :] = vn_ref[...]

    nk = kh_ref.shape[0] + kn_ref.shape[0]
    group = N_HEADS // N_KV
    rows = group * tq
    slopes = _alibi_slopes()

    def per_head(shape, head_in_group, vals):
        out = jnp.full(shape, vals[group - 1], F32)
        for g in range(group - 2, -1, -1):
            out = jnp.where(head_in_group == g, vals[g], out)
        return out

    @pl.when((pl.program_id(0) == 0) & (pl.program_id(1) == 0))
    def _():
        s = lax.broadcasted_iota(jnp.int32, (nk, rows), 0)
        col = lax.broadcasted_iota(jnp.int32, (nk, rows), 1)
        t = col & (tq - 1)
        neg_dist = -jnp.abs(t + (WINDOW - s)).astype(F32)
        for kh in range(N_KV):
            slope = per_head((nk, rows), col // tq, [float(slopes[kh * group + g]) for g in range(group)])
            bias_scr[kh] = slope * neg_dist
        for cap_scr in maybe_cap_scr:
            q_chunk, k_chunk = t // CHUNK, s // CHUNK
            in_band = (k_chunk >= q_chunk) & (k_chunk <= q_chunk + WINDOW // CHUNK)
            for first_block in range(2):
                visible = in_band & (k_chunk >= first_block * (WINDOW // CHUNK))
                cap_scr[first_block] = jnp.where(visible, jnp.inf, NEG_INF)

    first_block = jnp.where(pl.program_id(1) == 0, 1, 0)
    head_of_lane = lax.broadcasted_iota(jnp.int32, (1, rows), 1) // tq

    for kh in range(N_KV):
        cols = slice(kh * HEAD_DIM, (kh + 1) * HEAD_DIM)
        k_all = jnp.concatenate([kh_ref[:, cols], kn_ref[:, cols]], axis=0).astype(BF16)
        q = jnp.concatenate([q_ref[:, h * HEAD_DIM:(h + 1) * HEAD_DIM]
                             for h in range(kh * group, (kh + 1) * group)], axis=0)
        q = (q * (HEAD_DIM ** -0.5)).astype(BF16)
        sc_scr[kh] = lax.dot_general(k_all, q, (((1,), (1,)), ((), ())), preferred_element_type=F32)
    for kh in range(N_KV):
        sink_row = per_head((1, rows), head_of_lane, [sink_ref[kh * group + g] for g in range(group)])
        for c0 in range(0, rows, LANES):
            lanes = slice(c0, c0 + LANES)
            s = sc_scr[kh, :, lanes] + bias_scr[kh, :, lanes]
            for cap_scr in maybe_cap_scr:
                s = jnp.minimum(s, cap_scr[first_block, :, lanes])
            sink = sink_row[:, lanes]
            m = jnp.maximum(jnp.max(s, axis=0, keepdims=True), sink)
            p = jnp.exp(s - m)
            denom = jnp.sum(p, axis=0, keepdims=True) + jnp.exp(sink - m)
            w_scr[kh, :, lanes] = (p * (1.0 / denom)).astype(BF16)
    for kh in range(N_KV):
        cols = slice(kh * HEAD_DIM, (kh + 1) * HEAD_DIM)
        v_all = jnp.concatenate([vh_ref[:, cols], vn_ref[:, cols]], axis=0).astype(BF16)
        out = lax.dot_general(w_scr[kh], v_all, (((0,), (0,)), ((), ())), preferred_element_type=F32)
        for g in range(group):
            h = kh * group + g
            o_acc[:, h * HEAD_DIM:(h + 1) * HEAD_DIM] = out[g * tq:(g + 1) * tq, :]
    o_ref[...] = o_acc[...].astype(o_ref.dtype)


def _mixers_ab(proj, row_map, q_col, k_hist, v_hist, hist_maps, new_maps, sinks, conv_w, conv_prev,
               *, tq, band_mask, grid):
    m = proj.shape[0]
    assert tq <= WINDOW
    nk, rows = WINDOW + tq, N_HEADS // N_KV * tq

    def proj_cols(col):
        return lambda b, c: (row_map(b, c), col)

    conv_specs = [pl.BlockSpec((tq, CONV_DIM), proj_cols(col)) for col in range(3)]
    return pl.pallas_call(
        functools.partial(_attn_kernel, tq=tq),
        grid=grid,
        in_specs=[pl.BlockSpec(memory_space=pltpu.SMEM),
                  pl.BlockSpec((tq, Q_DIM), proj_cols(q_col)),
                  pl.BlockSpec((WINDOW, KV_DIM), hist_maps[0]), pl.BlockSpec((tq, KV_DIM), new_maps[0]),
                  pl.BlockSpec((WINDOW, KV_DIM), hist_maps[1]), pl.BlockSpec((tq, KV_DIM), new_maps[1])]
                 + conv_specs
                 + [pl.BlockSpec((3, CONV_DIM), lambda b, c: (0, 0)),
                    pl.BlockSpec((1, 2, CONV_DIM), lambda b, c: (b, 0, 0))],
        out_specs=[pl.BlockSpec((tq, Q_DIM), proj_cols(0)),
                   pl.BlockSpec((tq, CONV_DIM), proj_cols(0)),
                   pl.BlockSpec((1, 2, CONV_DIM), lambda b, c: (b, 0, 0)),
                   pl.BlockSpec((WINDOW, KV_DIM), lambda b, c: (b, 0)),
                   pl.BlockSpec((WINDOW, KV_DIM), lambda b, c: (b, 0))],
        out_shape=[jax.ShapeDtypeStruct((m, Q_DIM), BF16),
                   jax.ShapeDtypeStruct((m, CONV_DIM), BF16),
                   jax.ShapeDtypeStruct((grid[0], 2, CONV_DIM), F32),
                   jax.ShapeDtypeStruct((grid[0] * WINDOW, KV_DIM), F32),
                   jax.ShapeDtypeStruct((grid[0] * WINDOW, KV_DIM), F32)],
        scratch_shapes=[pltpu.VMEM((tq, Q_DIM), F32)]
                       + [pltpu.VMEM((N_KV, nk, rows), F32)] * 2
                       + [pltpu.VMEM((N_KV, nk, rows), BF16)]
                       + [pltpu.VMEM((2, CONV_DIM), F32)]
                       + ([pltpu.VMEM((2, nk, rows), F32)] if band_mask else []),
        compiler_params=_params("arbitrary", "arbitrary"),
        name="swa",
    )(sinks, proj, k_hist, proj, v_hist, proj, proj, proj, proj, conv_w, conv_prev)


def _gmlp_kernel(u_ref, v_ref, lng_ref, lnb_ref, ws_ref, bs_ref, o_ref, *maybe_vn_ref, cl):
    v = v_ref[...]
    mu = jnp.mean(v, axis=-1, keepdims=True)
    var = jnp.mean(jnp.square(v - mu), axis=-1, keepdims=True)
    vn = (v - mu) * lax.rsqrt(var + LN_EPS) * lng_ref[...] + lnb_ref[...]
    for vn_ref in maybe_vn_ref:
        vn_ref[...] = vn
    row = lax.broadcasted_iota(jnp.int32, (cl, cl), 0)
    col = lax.broadcasted_iota(jnp.int32, (cl, cl), 1)
    lower = row >= col
    for g in range(GMLP_GROUPS):
        cols = slice(g * GMLP_GDIM, (g + 1) * GMLP_GDIM)
        ws = jnp.where(lower, ws_ref[g, 0:cl, 0:cl], 0.0).astype(BF16)
        s = _dot(ws, vn[:, cols].astype(BF16)) + bs_ref[0:cl, g:g + 1]
        o_ref[:, cols] = (u_ref[:, cols] * s).astype(o_ref.dtype)


def _gmlp_gate(z, ln_g, ln_b, w_s, b_s_t, cl, want_vn):
    m = z.shape[0]
    n_out = 2 if want_vn else 1
    return pl.pallas_call(
        functools.partial(_gmlp_kernel, cl=cl),
        grid=(m // cl,),
        in_specs=[pl.BlockSpec((cl, D_MODEL), lambda i: (i, 0)),
                  pl.BlockSpec((cl, D_MODEL), lambda i: (i, 1)),
                  pl.BlockSpec((1, D_MODEL), lambda i: (0, 0)),
                  pl.BlockSpec((1, D_MODEL), lambda i: (0, 0)),
                  pl.BlockSpec((GMLP_GROUPS, GMLP_CHUNK, GMLP_CHUNK), lambda i: (0, 0, 0)),
                  pl.BlockSpec((GMLP_CHUNK, GMLP_GROUPS), lambda i: (0, 0))],
        out_specs=[pl.BlockSpec((cl, D_MODEL), lambda i: (i, 0)),
                   pl.BlockSpec((cl, D_MODEL), lambda i: (i, 0))][:n_out],
        out_shape=[jax.ShapeDtypeStruct((m, D_MODEL), BF16),
                   jax.ShapeDtypeStruct((m, D_MODEL), F32)][:n_out],
        compiler_params=_params("parallel"),
        name="gmlp_gate",
    )(z, z, ln_g.reshape(1, -1), ln_b.reshape(1, -1), w_s, b_s_t)


def _ffn_up_kernel(h_ref, wg_ref, wu_ref, cwg_ref, cwu_ref, bg_ref, bu_ref, pg_ref, pu_ref, *rest,
                   seq_len, cast_w_down):
    rest = list(rest)
    wd_ref = rest.pop(0) if cast_w_down else None
    act_ref, sg_ref, su_ref = rest.pop(0), rest.pop(0), rest.pop(0)
    wd_bf16_ref = rest.pop(0) if cast_w_down else None
    ug_scr, uu_scr = rest
    if cast_w_down:
        wd_bf16_ref[...] = wd_ref[...].astype(BF16)

    wg, wu = _as_bf16(wg_ref[...]), _as_bf16(wu_ref[...])
    ug_scr[:, 0:SCR_PITCH * HALO, :] = jnp.zeros((N_SLAB, SCR_PITCH * HALO, LANES), F32)
    uu_scr[:, 0:SCR_PITCH * HALO, :] = jnp.zeros((N_SLAB, SCR_PITCH * HALO, LANES), F32)
    row = lax.broadcasted_iota(jnp.int32, (EP_ROWS, 1), 0)

    def rows_at(scr, s, t, n):
        return scr[s, pl.ds(SCR_PITCH * (HALO + t), n, stride=SCR_PITCH), :]

    def conv(scr, prev_ref, cw_ref, b_ref, s, start, z):
        lanes = slice(s * LANES, (s + 1) * LANES)
        x0, x1, x2 = (rows_at(scr, s, start - back, EP_ROWS) for back in range(3))
        if start % seq_len == 0:
            prev = prev_ref[start // seq_len]
            x1 = jnp.where(row == 0, prev[1:2, lanes], x1)
            x2 = jnp.where(row == 0, prev[0:1, lanes], jnp.where(row == 1, prev[1:2, lanes], x2))

        def tap(x, w):
            return (x.reshape(EP_ROWS // SUBLANES, SUBLANES, LANES) * (w + z)[None]).reshape(EP_ROWS, LANES)

        return (tap(x2, cw_ref[0:1, lanes]) + tap(x1, cw_ref[1:2, lanes]) + tap(x0, cw_ref[2:3, lanes])
                + b_ref[:, lanes])

    zs = [jnp.zeros((SUBLANES, LANES), F32)] * N_SLAB

    row0 = 0
    for sub_rows in FFN_SUB_TILES:
        h = h_ref[row0:row0 + sub_rows, :]
        up_g, up_u = _dot(h, wg), _dot(h, wu)
        for s in range(N_SLAB):
            dst = pl.ds(SCR_PITCH * (HALO + row0), sub_rows, stride=SCR_PITCH)
            ug_scr[s, dst, :] = up_g[:, s * LANES:(s + 1) * LANES]
            uu_scr[s, dst, :] = up_u[:, s * LANES:(s + 1) * LANES]
        chunk_starts = range(row0, row0 + sub_rows, EP_ROWS)
        row0 += sub_rows
        for start in chunk_starts:
            end = start + EP_ROWS
            for s in range(N_SLAB):
                lanes = slice(s * LANES, (s + 1) * LANES)
                cg = conv(ug_scr, pg_ref, cwg_ref, bg_ref, s, start, zs[s])
                cu = conv(uu_scr, pu_ref, cwu_ref, bu_ref, s, start, zs[s])
                act = cg * jax.nn.sigmoid(cg) * cu
                act_ref[start:end, lanes] = act.astype(act_ref.dtype)
                bits = lax.bitcast_convert_type(act[0:SUBLANES, :], jnp.uint32)
                zs[s] = ((bits >> 16) >> 16).astype(F32)
                if end % seq_len == 0:
                    sg_ref[end // seq_len - 1, :, lanes] = rows_at(ug_scr, s, end - 2, 2)
                    su_ref[end // seq_len - 1, :, lanes] = rows_at(uu_scr, s, end - 2, 2)


def _ffn_up(h, w_up, conv_w, conv_b, layer, prev, prev_layer, seq_len, w_down=None):
    m = h.shape[0]
    assert sum(FFN_SUB_TILES) == ROWS and all(t % EP_ROWS == 0 for t in FFN_SUB_TILES)
    assert min(seq_len, ROWS) % EP_ROWS == 0 and ROWS % seq_len == 0
    nseq_tile = ROWS // seq_len
    nj = N_FF_TILES
    st_spec_g = pl.BlockSpec((None, nseq_tile, 2, FF_COLS), lambda i, j: (prev_layer, i, 0, j))
    st_spec_u = pl.BlockSpec((None, nseq_tile, 2, FF_COLS), lambda i, j: (prev_layer, i, 0, j + nj))
    conv_b = conv_b.reshape(conv_b.shape[0], 1, -1)
    in_specs = [pl.BlockSpec((ROWS, D_MODEL), lambda i, j: (i, 0), pipeline_mode=pl.Buffered(1)),
                pl.BlockSpec((None, D_MODEL, FF_COLS), lambda i, j: (layer, 0, j)),
                pl.BlockSpec((None, D_MODEL, FF_COLS), lambda i, j: (layer, 0, j + nj)),
                pl.BlockSpec((None, 3, FF_COLS), lambda i, j: (layer, 0, j)),
                pl.BlockSpec((None, 3, FF_COLS), lambda i, j: (layer, 0, j + nj)),
                pl.BlockSpec((None, 1, FF_COLS), lambda i, j: (layer, 0, j)),
                pl.BlockSpec((None, 1, FF_COLS), lambda i, j: (layer, 0, j + nj)),
                st_spec_g, st_spec_u]
    args = [h, w_up, w_up, conv_w, conv_w, conv_b, conv_b, prev, prev]
    out_specs = [pl.BlockSpec((ROWS, FF_COLS), lambda i, j: (i, j)),
                 pl.BlockSpec((nseq_tile, 2, FF_COLS), lambda i, j: (i, 0, j)),
                 pl.BlockSpec((nseq_tile, 2, FF_COLS), lambda i, j: (i, 0, j))]
    out_shape = [jax.ShapeDtypeStruct((m, D_FF), BF16),
                 jax.ShapeDtypeStruct((m // seq_len, 2, D_FF), F32),
                 jax.ShapeDtypeStruct((m // seq_len, 2, D_FF), F32)]
    if w_down is not None:
        n_steps = (m // ROWS) * nj
        cast_rows = D_FF // n_steps
        assert cast_rows * n_steps == D_FF and cast_rows % BF16_SUBLANES == 0
        in_specs.append(pl.BlockSpec((None, cast_rows, D_MODEL), lambda i, j: (layer, i * nj + j, 0)))
        args.append(w_down)
        out_specs.append(pl.BlockSpec((cast_rows, D_MODEL), lambda i, j: (i * nj + j, 0)))
        out_shape.append(jax.ShapeDtypeStruct((D_FF, D_MODEL), BF16))
    return pl.pallas_call(
        functools.partial(_ffn_up_kernel, seq_len=seq_len, cast_w_down=w_down is not None),
        grid=(m // ROWS, nj),
        in_specs=in_specs,
        out_specs=out_specs,
        out_shape=out_shape,
        scratch_shapes=[pltpu.VMEM((N_SLAB, SCR_PITCH * (HALO + ROWS), LANES), F32)] * 2,
        compiler_params=_params("parallel", "arbitrary"),
        name="ffn_up",
    )(*args)


def _trunk(x3, conv_a_cache, swa_k_cache, swa_v_cache, ffn_cache, wts, w_down_bf16):
    bsz, seq_len, _ = x3.shape
    m = bsz * seq_len
    sample = ffn_cache is not None
    x = x3.reshape(m, D_MODEL)

    h = _rmsnorm(x, wts["norm_mix"][0], BF16)
    proj = _matmul([h], wts["w_in_ab"], 0, **MM_F32W, name="in_ab")
    prev_a = conv_a_cache[0] if sample else jnp.zeros((bsz, 2, CONV_DIM), F32)

    q_col = 3 * CONV_DIM // Q_DIM
    k_col = (3 * CONV_DIM + Q_DIM) // KV_DIM
    v_col = k_col + 1
    if sample:
        k_hist = swa_k_cache[0].reshape(bsz * WINDOW, KV_DIM)
        v_hist = swa_v_cache[0].reshape(bsz * WINDOW, KV_DIM)
        hist = [lambda b, c: (b, 0)] * 2
        new = [lambda b, c: (b, k_col), lambda b, c: (b, v_col)]
        attn, a, conv_state, ks, vs = _mixers_ab(proj, lambda b, c: b, q_col, k_hist, v_hist, hist, new,
                                                 wts["sinks"][0], wts["conv_w_a"][0], prev_a,
                                                 tq=seq_len, band_mask=False, grid=(bsz, 1))
    else:
        nblk = seq_len // WINDOW
        def blk(back, col):
            return lambda b, c: (b * nblk + jnp.maximum(c - back, 0), col)
        attn, a, conv_state, ks, vs = _mixers_ab(proj, lambda b, c: b * nblk + c, q_col, proj, proj,
                                                 [blk(1, k_col), blk(1, v_col)], [blk(0, k_col), blk(0, v_col)],
                                                 wts["sinks"][0], wts["conv_w_a"][0], prev_a,
                                                 tq=WINDOW, band_mask=True, grid=(bsz, nblk))

    k_state = ks.reshape(bsz, WINDOW, N_KV, HEAD_DIM)
    v_state = vs.reshape(bsz, WINDOW, N_KV, HEAD_DIM)

    x = _matmul([a, attn], wts["w_out_ab"], 0, **MM_F32W, epilogue="residual", residual=x,
                name="out_ab")

    ffn_states = []

    def conv_ffn(x, layer, next_gain):
        h = _rmsnorm(x, wts["norm_ffn"][layer], BF16)
        prev, prev_layer = (ffn_cache, layer) if sample else (jnp.zeros((1, bsz, 2, 2 * D_FF), F32), 0)
        act, st_g, st_u, *maybe_wd = _ffn_up(h, wts["w_up_ffn"], wts["conv_w_ffn"], wts["conv_b_ffn"], layer,
                                             prev, prev_layer, seq_len,
                                             w_down=None if sample else wts["w_down_ffn"])
        if not sample:
            w_down_bf16[layer] = maybe_wd[0][None]
        ffn_states.append(jnp.concatenate([st_g, st_u], axis=-1))
        return _matmul([act], w_down_bf16[layer], 0, **MM_DOWN, epilogue="residual",
                       residual=x, next_gain=next_gain, name="ffn_down")

    x, xg, ss = conv_ffn(x, 0, wts["norm_mix"][1])

    z = _matmul([xg], wts["w_in_c"], 0, **MM_F32W, epilogue="gelu", row_ss=ss, name="in_c")
    cl = min(seq_len, GMLP_CHUNK)
    gated, *maybe_vn = _gmlp_gate(z, wts["ln_g_c"][0], wts["ln_b_c"][0], wts["w_s_c"][0], wts["b_s_c"][0].T,
                                  cl, want_vn=sample)
    gv_state = maybe_vn[0].reshape(1, bsz, seq_len, D_MODEL) if sample else None
    x = _matmul([gated], wts["w_out_c"], 0, **MM_F32W, epilogue="residual", residual=x,
                name="out_c")
    x = conv_ffn(x, 1, None)

    y = _rmsnorm(x, wts["norm_final"], F32)
    return (y.reshape(bsz, seq_len, D_MODEL), conv_state[None], k_state[None], v_state[None],
            jnp.stack(ffn_states), gv_state)


def kernel(x_prompt, x_sample, state_conv_a, cache_swa_k, cache_swa_v, state_ffn_conv, norm_mix, norm_ffn,
           norm_final, w_in_ab, conv_w_a, sinks, w_out_ab, w_in_c, ln_g_c, ln_b_c, w_s_c, b_s_c, w_out_c,
           w_up_ffn, conv_w_ffn, conv_b_ffn, w_down_ffn):
    wts = dict(norm_mix=norm_mix, norm_ffn=norm_ffn, norm_final=norm_final,
               w_in_ab=w_in_ab, conv_w_a=conv_w_a, sinks=sinks, w_out_ab=w_out_ab,
               w_in_c=w_in_c, ln_g_c=ln_g_c, ln_b_c=ln_b_c, w_s_c=w_s_c, b_s_c=b_s_c,
               w_out_c=w_out_c, w_up_ffn=w_up_ffn, conv_w_ffn=conv_w_ffn,
               conv_b_ffn=conv_b_ffn, w_down_ffn=w_down_ffn)
    w_down_bf16 = {}
    y_p, conv_p, k_p, v_p, ffn_p, _ = _trunk(x_prompt, None, None, None, None, wts, w_down_bf16)
    y_s, conv_s, k_s, v_s, ffn_s, gv_s = _trunk(x_sample, state_conv_a, cache_swa_k, cache_swa_v,
                                                state_ffn_conv, wts, w_down_bf16)
    return (y_p, y_s, conv_p, conv_s, k_p, k_s, v_p, v_s, ffn_p, ffn_s, gv_s)
```
